```python
import functools
import jax, jax.numpy as jnp
from jax import lax
import numpy as np

D_MODEL = 1024
BATCH = 8
SEQ = 4096
DEPTH = 1
DEC_BATCH = 128
DEC_SEQ = 4
PAST_LEN = 8192
PAGE_SIZE = 128

D_MIX = D_MODEL
LRU_WIDTH = D_MIX // 2
LRU_BLOCKS = 8
LRU_BLOCK = LRU_WIDTH // LRU_BLOCKS
CONV_WIDTH = 4
LRU_C = 8.0
FOX_HEADS = 8
HEAD_DIM = (D_MIX - LRU_WIDTH) // FOX_HEADS
FOX_WIDTH = FOX_HEADS * HEAD_DIM
D_FF = 3 * D_MODEL
FFN_CONV = 3
PLE_DIM = 256
Q_BLOCK = 128
EPS = 1e-6
FORGET_BIAS = 6.0
D_IN = 2 * LRU_WIDTH + 3 * FOX_WIDTH + FOX_HEADS
SPLITS = [LRU_WIDTH, 2 * LRU_WIDTH, 2 * LRU_WIDTH + FOX_WIDTH,
          2 * LRU_WIDTH + 2 * FOX_WIDTH, 2 * LRU_WIDTH + 3 * FOX_WIDTH]

kernel_name = "hymba_rglru_fox_convffn_ple_step"


def _rmsnorm(x, g):
    xf = x.astype(jnp.float32)
    y = xf * lax.rsqrt(jnp.mean(xf * xf, axis=-1, keepdims=True) + EPS)
    return (y * g.astype(jnp.float32)).astype(x.dtype)


def _causal_dwconv(x, buf, w, b):
    full = jnp.concatenate([buf.astype(x.dtype), x], axis=1)
    width = w.shape[0]
    t = x.shape[1]
    out = sum(w[j] * full[:, j:j + t] for j in range(width)) + b
    tail = full[:, full.shape[1] - (width - 1):]
    return out, tail


def _rglru(xc, h0, reset, w_a, b_a, w_x, b_x, lam):
    bsz, t, wd = xc.shape
    xb = xc.reshape(bsz, t, LRU_BLOCKS, LRU_BLOCK)
    r = jax.nn.sigmoid(jnp.einsum('btnk,nkj->btnj', xb, w_a) + b_a).reshape(bsz, t, wd)
    i = jax.nn.sigmoid(jnp.einsum('btnk,nkj->btnj', xb, w_x) + b_x).reshape(bsz, t, wd)
    log_a = -LRU_C * r.astype(jnp.float32) * jax.nn.softplus(-lam.astype(jnp.float32))
    a = jnp.exp(log_a)
    mult = jnp.where(reset[None, :, None], 1.0, jnp.sqrt(-jnp.expm1(2.0 * log_a)))
    u = mult * (i * xc).astype(jnp.float32)

    def step(h, au):
        a_t, u_t = au
        h = a_t * h + u_t
        return h, h

    h_t, hs = lax.scan(step, h0.astype(jnp.float32),
                       (jnp.swapaxes(a, 0, 1), jnp.swapaxes(u, 0, 1)))
    return jnp.swapaxes(hs, 0, 1).astype(xc.dtype), h_t.astype(xc.dtype)


def _fox_prompt(q, k, v, logf):
    bsz, s_len, nh, hd = q.shape
    scale = hd ** -0.5
    c = jnp.swapaxes(lax.cumsum(logf, axis=1), 1, 2)
    nb = s_len // Q_BLOCK
    qb = q.reshape(bsz, nb, Q_BLOCK, nh, hd).transpose(1, 0, 2, 3, 4)
    cqb = c.reshape(bsz, nh, nb, Q_BLOCK).transpose(2, 0, 1, 3)
    kpos = jnp.arange(s_len)

    def block(args):
        qi, cq, bi = args
        qpos = bi * Q_BLOCK + jnp.arange(Q_BLOCK)
        sc = jnp.einsum('bqhd,bkhd->bhqk', qi, k).astype(jnp.float32) * scale
        sc = sc + cq[..., None] - c[:, :, None, :]
        sc = jnp.where(kpos[None, :] <= qpos[:, None], sc, -jnp.inf)
        p = jax.nn.softmax(sc, axis=-1)
        return jnp.einsum('bhqk,bkhd->bqhd', p.astype(v.dtype), v)

    out = lax.map(block, (qb, cqb, jnp.arange(nb)))
    return out.transpose(1, 0, 2, 3, 4).reshape(bsz, s_len, nh * hd)


def _fox_sample(q, k, v, logf, k_past, v_past, logf_past):
    bsz, t, nh, hd = q.shape
    p_len = k_past.shape[1]
    scale = hd ** -0.5
    cn = jnp.swapaxes(lax.cumsum(logf, axis=1), 1, 2)
    lp = jnp.swapaxes(logf_past.astype(jnp.float32), 1, 2)
    rc = lax.cumsum(lp, axis=2, reverse=True) - lp
    s_past = jnp.einsum('bqhd,bkhd->bhqk', q, k_past).astype(jnp.float32) * scale
    s_past = s_past + cn[..., None] + rc[:, :, None, :]
    s_new = jnp.einsum('bqhd,bkhd->bhqk', q, k).astype(jnp.float32) * scale
    s_new = s_new + cn[..., None] - cn[:, :, None, :]
    tpos = jnp.arange(t)
    s_new = jnp.where(tpos[None, :] <= tpos[:, None], s_new, -jnp.inf)
    p = jax.nn.softmax(jnp.concatenate([s_past, s_new], axis=-1), axis=-1)
    p_past, p_new = p[..., :p_len], p[..., p_len:]
    out = (jnp.einsum('bhqk,bkhd->bqhd', p_past.astype(v.dtype), v_past)
           + jnp.einsum('bhqk,bkhd->bqhd', p_new.astype(v.dtype), v))
    return out.reshape(bsz, t, nh * hd)


def _layer(h, p_l, reset, attend, lru_h0, lru_buf, ffn_buf,
           g_mix, w_in, lru_conv_w, lru_conv_b, lru_w_a, lru_b_a, lru_w_x, lru_b_x,
           lru_lambda, fox_b_f, g_lru_out, g_fox_out, w_out, g_ffn, w_up,
           ffn_conv_w, ffn_conv_b, w_down, w_ple, w_ple_gate):
    bsz, t, _ = h.shape
    n = _rmsnorm(h, g_mix)
    proj = n @ w_in
    xr, gr, q, k, v, fl = jnp.split(proj, SPLITS, axis=-1)
    xc, lru_tail = _causal_dwconv(xr, lru_buf, lru_conv_w, lru_conv_b)
    y_lru, lru_ht = _rglru(xc, lru_h0, reset, lru_w_a, lru_b_a, lru_w_x, lru_b_x, lru_lambda)
    y_lru = y_lru * jax.nn.gelu(gr)
    q = q.reshape(bsz, t, FOX_HEADS, HEAD_DIM)
    k = k.reshape(bsz, t, FOX_HEADS, HEAD_DIM)
    v = v.reshape(bsz, t, FOX_HEADS, HEAD_DIM)
    logf = jax.nn.log_sigmoid((fl + fox_b_f).astype(jnp.float32))
    y_fox = attend(q, k, v, logf)
    mix = jnp.concatenate([_rmsnorm(y_lru, g_lru_out), _rmsnorm(y_fox, g_fox_out)], axis=-1)
    h = h + mix @ w_out
    n2 = _rmsnorm(h, g_ffn)
    g, u = jnp.split(n2 @ w_up, 2, axis=-1)
    g, ffn_tail = _causal_dwconv(g, ffn_buf, ffn_conv_w, ffn_conv_b)
    h = h + (jax.nn.gelu(g) * u) @ w_down
    h = h + (p_l @ w_ple) * jax.nn.sigmoid(h @ w_ple_gate)
    return h, (k, v, logf, lru_ht, lru_tail, ffn_tail)


def setup_inputs(seed: int = 0) -> dict:
    key = jax.random.key(seed)
    ks = iter(jax.random.split(key, 48))
    f32 = jnp.float32
    n_pages = PAST_LEN // PAGE_SIZE
    n_used = DEC_BATCH * n_pages
    n_phys = n_used + max(1, n_used // 4)

    def nrm(shape, scale):
        return scale * jax.random.normal(next(ks), shape, f32)

    def gain(shape):
        return 1.0 + 0.05 * jax.random.normal(next(ks), shape, f32)

    x_prompt = nrm((BATCH, SEQ, D_MODEL), 1.0)
    x_sample = nrm((DEC_BATCH, DEC_SEQ, D_MODEL), 1.0)
    p_prompt = nrm((DEPTH, BATCH, SEQ, PLE_DIM), 1.0)
    p_sample = nrm((DEPTH, DEC_BATCH, DEC_SEQ, PLE_DIM), 1.0)
    cache_k = nrm((DEPTH, n_phys, PAGE_SIZE, FOX_HEADS, HEAD_DIM), 1.0)
    cache_v = nrm((DEPTH, n_phys, PAGE_SIZE, FOX_HEADS, HEAD_DIM), 1.0)
    cache_logf = jax.nn.log_sigmoid(9.0 + nrm((DEPTH, n_phys, PAGE_SIZE, FOX_HEADS), 0.5))
    state_lru_h = nrm((DEPTH, DEC_BATCH, LRU_WIDTH), 0.5)
    state_lru_conv = nrm((DEPTH, DEC_BATCH, CONV_WIDTH - 1, LRU_WIDTH), 1.0)
    state_ffn_conv = nrm((DEPTH, DEC_BATCH, FFN_CONV - 1, D_FF), 1.0)
    perm = jax.random.permutation(next(ks), n_phys)
    page_table = perm[:n_used].reshape(DEC_BATCH, n_pages).astype(jnp.int32)
    a_init = jax.random.uniform(next(ks), (DEPTH, LRU_WIDTH), f32, 0.9, 0.999)
    lru_lambda = jnp.log(a_init) - jnp.log1p(-a_init)
    return {
        'x_prompt': x_prompt, 'x_sample': x_sample,
        'p_prompt': p_prompt, 'p_sample': p_sample,
        'cache_k': cache_k, 'cache_v': cache_v, 'cache_logf': cache_logf,
        'state_lru_h': state_lru_h, 'state_lru_conv': state_lru_conv,
        'state_ffn_conv': state_ffn_conv, 'page_table': page_table,
        'g_mix': gain((DEPTH, D_MODEL)),
        'w_in': nrm((DEPTH, D_MODEL, D_IN), D_MODEL ** -0.5),
        'lru_conv_w': nrm((DEPTH, CONV_WIDTH, LRU_WIDTH), CONV_WIDTH ** -0.5),
        'lru_conv_b': nrm((DEPTH, LRU_WIDTH), 0.01),
        'lru_w_a': nrm((DEPTH, LRU_BLOCKS, LRU_BLOCK, LRU_BLOCK), LRU_BLOCK ** -0.5),
        'lru_b_a': nrm((DEPTH, LRU_BLOCKS, LRU_BLOCK), 0.01),
        'lru_w_x': nrm((DEPTH, LRU_BLOCKS, LRU_BLOCK, LRU_BLOCK), LRU_BLOCK ** -0.5),
        'lru_b_x': nrm((DEPTH, LRU_BLOCKS, LRU_BLOCK), 0.01),
        'lru_lambda': lru_lambda,
        'fox_b_f': FORGET_BIAS + nrm((DEPTH, FOX_HEADS), 0.1),
        'g_lru_out': gain((DEPTH, LRU_WIDTH)),
        'g_fox_out': gain((DEPTH, FOX_WIDTH)),
        'w_out': nrm((DEPTH, D_MIX, D_MODEL), D_MIX ** -0.5),
        'g_ffn': gain((DEPTH, D_MODEL)),
        'w_up': nrm((DEPTH, D_MODEL, 2 * D_FF), D_MODEL ** -0.5),
        'ffn_conv_w': nrm((DEPTH, FFN_CONV, D_FF), FFN_CONV ** -0.5),
        'ffn_conv_b': nrm((DEPTH, D_FF), 0.01),
        'w_down': nrm((DEPTH, D_FF, D_MODEL), D_FF ** -0.5),
        'w_ple': nrm((DEPTH, PLE_DIM, D_MODEL), PLE_DIM ** -0.5),
        'w_ple_gate': nrm((DEPTH, D_MODEL, D_MODEL), D_MODEL ** -0.5),
        'g_final': gain((D_MODEL,)),
    }


def reference(x_prompt, x_sample, p_prompt, p_sample, cache_k, cache_v, cache_logf,
              state_lru_h, state_lru_conv, state_ffn_conv, page_table,
              g_mix, w_in, lru_conv_w, lru_conv_b, lru_w_a, lru_b_a, lru_w_x, lru_b_x,
              lru_lambda, fox_b_f, g_lru_out, g_fox_out, w_out, g_ffn, w_up,
              ffn_conv_w, ffn_conv_b, w_down, w_ple, w_ple_gate, g_final):
    bp, sp, _ = x_prompt.shape
    bs, ts, _ = x_sample.shape
    db, n_pages = page_table.shape
    reset_p = jnp.arange(sp) == 0
    reset_s = (PAST_LEN + jnp.arange(ts)) == 0
    hp, hs = x_prompt, x_sample
    outs_p, outs_s = [], []
    for l in range(DEPTH):
        lw = (g_mix[l], w_in[l], lru_conv_w[l], lru_conv_b[l], lru_w_a[l], lru_b_a[l],
              lru_w_x[l], lru_b_x[l], lru_lambda[l], fox_b_f[l], g_lru_out[l], g_fox_out[l],
              w_out[l], g_ffn[l], w_up[l], ffn_conv_w[l], ffn_conv_b[l], w_down[l],
              w_ple[l], w_ple_gate[l])
        hp, st_p = _layer(hp, p_prompt[l], reset_p, _fox_prompt,
                          jnp.zeros((bp, LRU_WIDTH), hp.dtype),
                          jnp.zeros((bp, CONV_WIDTH - 1, LRU_WIDTH), hp.dtype),
                          jnp.zeros((bp, FFN_CONV - 1, D_FF), hp.dtype), *lw)
        outs_p.append(st_p)
        k_past = cache_k[l][page_table].reshape(db, n_pages * PAGE_SIZE, FOX_HEADS, HEAD_DIM)
        v_past = cache_v[l][page_table].reshape(db, n_pages * PAGE_SIZE, FOX_HEADS, HEAD_DIM)
        lf_past = cache_logf[l][page_table].reshape(db, n_pages * PAGE_SIZE, FOX_HEADS)
        attend_s = functools.partial(_fox_sample, k_past=k_past, v_past=v_past, logf_past=lf_past)
        hs, st_s = _layer(hs, p_sample[l], reset_s, attend_s,
                          state_lru_h[l], state_lru_conv[l], state_ffn_conv[l], *lw)
        outs_s.append(st_s)
    y_prompt = _rmsnorm(hp, g_final)
    y_sample = _rmsnorm(hs, g_final)
    k_p = jnp.stack([o[0] for o in outs_p]); v_p = jnp.stack([o[1] for o in outs_p])
    lf_p = jnp.stack([o[2] for o in outs_p]); h_p = jnp.stack([o[3] for o in outs_p])
    cx_p = jnp.stack([o[4] for o in outs_p]); cf_p = jnp.stack([o[5] for o in outs_p])
    k_s = jnp.stack([o[0] for o in outs_s]); v_s = jnp.stack([o[1] for o in outs_s])
    lf_s = jnp.stack([o[2] for o in outs_s]); h_s = jnp.stack([o[3] for o in outs_s])
    cx_s = jnp.stack([o[4] for o in outs_s]); cf_s = jnp.stack([o[5] for o in outs_s])
    return (y_prompt, y_sample, k_p, v_p, lf_p, h_p, cx_p, cf_p, k_s, v_s, lf_s, h_s, cx_s, cf_s)
```

```python
import functools
import math

import jax
import jax.numpy as jnp
from jax import lax
from jax.experimental import pallas as pl
from jax.experimental.pallas import tpu as pltpu

F32 = jnp.float32
BF16 = jnp.bfloat16

EPS = 1e-6
LRU_C = 8.0
HEAD_DIM = 64
HEADS = 8
LANES = 128
SUBLANES = 8
VMEM_LIMIT = 56 * 1024 * 1024
NEG_INF = float("-inf")
GELU_C = math.sqrt(2.0 / math.pi)


def _rms(x, g):
    ms = jnp.mean(x * x, axis=-1, keepdims=True)
    return x * lax.rsqrt(ms + EPS) * g


def _gelu(x):
    return x * (0.5 * (1.0 + jnp.tanh(GELU_C * (x + 0.044715 * (x * x * x)))))


def _softplus(x):
    return jnp.maximum(x, 0.0) + jnp.log1p(jnp.exp(-jnp.abs(x)))


def _dot(a, b):
    return jnp.dot(a, b, preferred_element_type=F32)


def _dot_nt(a, b):
    return lax.dot_general(a, b, (((1,), (1,)), ((), ())), preferred_element_type=F32)


def _const_spec(shape):
    nd = len(shape)
    return pl.BlockSpec(shape, lambda *_: (0,) * nd, pipeline_mode=pl.Buffered(1))


def _whole_spec(shape):
    nd = len(shape)
    return pl.BlockSpec(shape, lambda *_: (0,) * nd)


def _time_major(ref, nt):
    c = ref.shape[1] // nt
    return jnp.concatenate([ref[:, t * c:(t + 1) * c] for t in range(nt)], axis=0)


def _inproj_kernel(x_ref, g_ref, wm_ref, wf_ref, bf_ref, *outs, nt, q_dtype, emit_bf16_kv):
    if emit_bf16_kv:
        xr_ref, gr_ref, q_ref, k_ref, v_ref, lf_ref, kb_ref, vb_ref = outs
    else:
        xr_ref, gr_ref, q_ref, k_ref, v_ref, lf_ref = outs
    d = x_ref.shape[1] // nt
    w = xr_ref.shape[1] // nt
    for t in range(nt):
        n = _rms(x_ref[:, t * d:(t + 1) * d], g_ref[...]).astype(BF16)
        ws = slice(t * w, (t + 1) * w)
        xr_ref[:, ws] = _dot(n, wm_ref[:, 0 * w:1 * w])
        gr_ref[:, ws] = _dot(n, wm_ref[:, 1 * w:2 * w])
        q_ref[:, ws] = (_dot(n, wm_ref[:, 2 * w:3 * w]) * (HEAD_DIM ** -0.5)).astype(q_dtype)
        k = _dot(n, wm_ref[:, 3 * w:4 * w])
        v = _dot(n, wm_ref[:, 4 * w:5 * w])
        k_ref[:, ws] = k
        v_ref[:, ws] = v
        if emit_bf16_kv:
            kb_ref[:, ws] = k.astype(BF16)
            vb_ref[:, ws] = v.astype(BF16)
        fl = _dot(n, wf_ref[...]) + bf_ref[...]
        lf_ref[:, t * LANES:(t + 1) * LANES] = -_softplus(-fl)


def _inproj(x, g_mix, w_main, w_f, b_f, *, tm, nt, q_dtype, emit_bf16_kv):
    rows = x.shape[0]
    w = w_main.shape[1] // 5
    row_spec = lambda c: pl.BlockSpec((tm, c), lambda i: (i, 0))
    out_shape = [jax.ShapeDtypeStruct((rows, nt * w), F32), jax.ShapeDtypeStruct((rows, nt * w), F32),
                 jax.ShapeDtypeStruct((rows, nt * w), q_dtype), jax.ShapeDtypeStruct((rows, nt * w), F32),
                 jax.ShapeDtypeStruct((rows, nt * w), F32), jax.ShapeDtypeStruct((rows, nt * LANES), F32)]
    out_specs = [row_spec(nt * w)] * 5 + [row_spec(nt * LANES)]
    if emit_bf16_kv:
        out_shape += [jax.ShapeDtypeStruct((rows, nt * w), BF16)] * 2
        out_specs += [row_spec(nt * w)] * 2
    return pl.pallas_call(
        functools.partial(_inproj_kernel, nt=nt, q_dtype=q_dtype, emit_bf16_kv=emit_bf16_kv),
        grid=(rows // tm,),
        in_specs=[row_spec(x.shape[1]), _const_spec(g_mix.shape), _const_spec(w_main.shape),
                  _const_spec(w_f.shape), _const_spec(b_f.shape)],
        out_specs=out_specs, out_shape=out_shape,
        compiler_params=pltpu.CompilerParams(dimension_semantics=("arbitrary",),
                                             vmem_limit_bytes=VMEM_LIMIT),
        name="inproj",
    )(x, g_mix, w_main, w_f, b_f)


def _cumsum_kernel(lf_ref, logf_ref, c_ref, ct_ref):
    x = lf_ref[...]
    logf_ref[...] = x[:, :HEADS]
    n = x.shape[0]
    row = lax.broadcasted_iota(jnp.int32, x.shape, 0)
    d = 1
    while d < n:
        x = x + jnp.where(row >= d, pltpu.roll(x, d, axis=0), 0.0)
        d *= 2
    c_ref[...] = x
    ct_ref[...] = x.T[:SUBLANES, :]


def _cumsum(lf_pad, batch, seq):
    rows = lf_pad.shape[0]
    return pl.pallas_call(
        _cumsum_kernel,
        grid=(batch,),
        in_specs=[pl.BlockSpec((seq, LANES), lambda b: (b, 0))],
        out_specs=[pl.BlockSpec((seq, HEADS), lambda b: (b, 0)),
                   pl.BlockSpec((seq, LANES), lambda b: (b, 0)),
                   pl.BlockSpec((SUBLANES, seq), lambda b: (0, b))],
        out_shape=[jax.ShapeDtypeStruct((rows, HEADS), F32),
                   jax.ShapeDtypeStruct((rows, LANES), F32),
                   jax.ShapeDtypeStruct((SUBLANES, rows), F32)],
        compiler_params=pltpu.CompilerParams(dimension_semantics=("arbitrary",),
                                             vmem_limit_bytes=VMEM_LIMIT),
        name="logf_cumsum",
    )(lf_pad)


def _causal_conv(ext_ref, cw_ref, cb_ref, cols, *, pad, unit, n):
    taps = cw_ref.shape[0]
    out = cb_ref[:, cols]
    for j in range(taps):
        out = out + cw_ref[taps - 1 - j:taps - j, cols] * ext_ref[pl.ds(pad - j * unit, n), :]
    return out


def _lru_gates(xc, wa_ref, ba_ref, wx_ref, bx_ref, lam_ref):
    xcb = xc.astype(BF16)
    r = jax.nn.sigmoid(_dot(xcb, wa_ref[...]) + ba_ref[...])
    i = jax.nn.sigmoid(_dot(xcb, wx_ref[...]) + bx_ref[...])
    log_a = (-LRU_C) * r * _softplus(-lam_ref[...])
    a = jnp.exp(log_a)
    mult = jnp.sqrt(-jnp.tanh(log_a) * (a * a + 1.0))
    return a, mult, i * xc


def _lru_prompt_kernel(xr_ref, gr_ref, cw_ref, cb_ref, wa_ref, ba_ref, wx_ref, bx_ref, lam_ref,
                       g_ref, y_ref, hlast_ref, ext_ref, a_ref, u_ref, hs_ref, h_ref, *, tt):
    t = pl.program_id(1)
    pad = SUBLANES

    @pl.when(t == 0)
    def _():
        ext_ref[0:pad, :] = jnp.zeros((pad, ext_ref.shape[1]), F32)
        h_ref[...] = jnp.zeros(h_ref.shape, F32)

    x = xr_ref[...]
    ext_ref[pad:pad + tt, :] = x
    xc = _causal_conv(ext_ref, cw_ref, cb_ref, slice(None), pad=pad, unit=1, n=tt)
    ext_ref[0:pad, :] = x[tt - pad:tt, :]

    a, mult, ix = _lru_gates(xc, wa_ref, ba_ref, wx_ref, bx_ref, lam_ref)
    row = lax.broadcasted_iota(jnp.int32, (tt, 1), 0)
    mult = jnp.where((row == 0) & (t == 0), 1.0, mult)
    a_ref[...] = a
    u_ref[...] = mult * ix

    row8 = lax.broadcasted_iota(jnp.int32, (SUBLANES, a_ref.shape[1]), 0)

    def group(gi, h):
        r0 = pl.multiple_of(gi * SUBLANES, SUBLANES)
        ag = a_ref[pl.ds(r0, SUBLANES), :]
        ug = u_ref[pl.ds(r0, SUBLANES), :]
        for d in (1, 2, 4):
            a_sh = jnp.where(row8 >= d, pltpu.roll(ag, d, axis=0), 1.0)
            u_sh = jnp.where(row8 >= d, pltpu.roll(ug, d, axis=0), 0.0)
            ug = ag * u_sh + ug
            ag = ag * a_sh
        hs = ag * h + ug
        hs_ref[pl.ds(r0, SUBLANES), :] = hs
        return jnp.broadcast_to(hs[SUBLANES - 1:SUBLANES, :], hs.shape)

    h = lax.fori_loop(0, tt // SUBLANES, group, h_ref[...], unroll=2)
    h_ref[...] = h
    hlast_ref[0] = h[0:1, :]
    y = hs_ref[...] * _gelu(gr_ref[...])
    y_ref[...] = _rms(y, g_ref[...]).astype(y_ref.dtype)


def _lru_prompt(xr, gr, cw, cb, wa, ba, wx, bx, lam, g_out, *, batch, seq, tt):
    rows, w = xr.shape
    nt = seq // tt
    tile = pl.BlockSpec((tt, w), lambda b, t: (b * nt + t, 0))
    consts = [cw, cb, wa, ba, wx, bx, lam, g_out]
    return pl.pallas_call(
        functools.partial(_lru_prompt_kernel, tt=tt),
        grid=(batch, nt),
        in_specs=[tile, tile] + [_const_spec(c.shape) for c in consts],
        out_specs=[tile, pl.BlockSpec((1, 1, w), lambda b, t: (b, 0, 0))],
        out_shape=[jax.ShapeDtypeStruct((rows, w), BF16), jax.ShapeDtypeStruct((batch, 1, w), F32)],
        scratch_shapes=[pltpu.VMEM((SUBLANES + tt, w), F32), pltpu.VMEM((tt, w), F32),
                        pltpu.VMEM((tt, w), F32), pltpu.VMEM((tt, w), F32),
                        pltpu.VMEM((SUBLANES, w), F32)],
        compiler_params=pltpu.CompilerParams(dimension_semantics=("arbitrary", "arbitrary"),
                                             vmem_limit_bytes=VMEM_LIMIT),
        name="lru_prompt",
    )(xr, gr, *consts)


def _lru_sample_kernel(xr_ref, gr_ref, h0_ref, cbuf_ref, cw_ref, cb_ref, wa_ref, ba_ref, wx_ref,
                       bx_ref, lam_ref, g_ref, y_ref, hlast_ref, ext_ref, *, nt):
    nb, w = h0_ref.shape
    taps = cw_ref.shape[0]
    pad = (taps - 1) * nb
    n = nb * nt
    ext_ref[0:pad, :] = _time_major(cbuf_ref, taps - 1)
    ext_ref[pad:pad + n, :] = _time_major(xr_ref, nt)
    xc = _causal_conv(ext_ref, cw_ref, cb_ref, slice(None), pad=pad, unit=nb, n=n)
    a, mult, ix = _lru_gates(xc, wa_ref, ba_ref, wx_ref, bx_ref, lam_ref)
    u = mult * ix
    h = h0_ref[...]
    for t in range(nt):
        h = a[t * nb:(t + 1) * nb, :] * h + u[t * nb:(t + 1) * nb, :]
        y = h * _gelu(gr_ref[:, t * w:(t + 1) * w])
        y_ref[:, t * w:(t + 1) * w] = _rms(y, g_ref[...]).astype(y_ref.dtype)
    hlast_ref[...] = h


def _lru_sample(xr, gr, h0, cbuf, cw, cb, wa, ba, wx, bx, lam, g_out, *, nt):
    nb, w = h0.shape
    args = [xr, gr, h0, cbuf, cw, cb, wa, ba, wx, bx, lam, g_out]
    taps = cw.shape[0]
    return pl.pallas_call(
        functools.partial(_lru_sample_kernel, nt=nt),
        grid=(1,),
        in_specs=[_const_spec(a.shape) for a in args],
        out_specs=[_whole_spec((nb, nt * w)), _whole_spec((nb, w))],
        out_shape=[jax.ShapeDtypeStruct((nb, nt * w), BF16), jax.ShapeDtypeStruct((nb, w), F32)],
        scratch_shapes=[pltpu.VMEM(((taps - 1 + nt) * nb, w), F32)],
        compiler_params=pltpu.CompilerParams(dimension_semantics=("arbitrary",),
                                             vmem_limit_bytes=VMEM_LIMIT),
        name="lru_sample",
    )(*args)


def _attn_prompt_kernel(q_ref, k_ref, v_ref, c_ref, ct_ref, o_ref, *, tq):
    hp = pl.program_id(1)
    qi = pl.program_id(2)
    q = q_ref[...]
    c_blk = c_ref[...]
    lane = lax.broadcasted_iota(jnp.int32, (tq, LANES), 1)
    row = lax.broadcasted_iota(jnp.int32, (tq, tq), 0)
    col = lax.broadcasted_iota(jnp.int32, (tq, tq), 1)
    outs = []
    for hh in range(LANES // HEAD_DIM):
        h = hp * (LANES // HEAD_DIM) + hh
        in_head = (lane >= hh * HEAD_DIM) & (lane < (hh + 1) * HEAD_DIM)
        qm = jnp.where(in_head, q, jnp.zeros_like(q))
        cq = jnp.sum(jnp.where(lane == h, c_blk, 0.0), axis=-1, keepdims=True)

        def tile(j, carry, masked):
            m, l, acc = carry
            ks = pl.multiple_of(j * tq, tq)
            z = _dot_nt(qm, k_ref[pl.ds(ks, tq), :]) - ct_ref[pl.ds(h, 1), pl.ds(ks, tq)]
            if masked:
                z = jnp.where(col <= row, z, NEG_INF)
            m_new = jnp.maximum(m, jnp.max(z, axis=-1, keepdims=True) + cq)
            p = jnp.exp(z - (m_new - cq))
            alpha = jnp.exp(m - m_new)
            l = alpha * l + jnp.sum(p, axis=-1, keepdims=True)
            acc = alpha * acc + _dot(p.astype(BF16), v_ref[pl.ds(ks, tq), :])
            return m_new, l, acc

        init = (jnp.full((tq, 1), NEG_INF, F32), jnp.zeros((tq, 1), F32), jnp.zeros((tq, LANES), F32))
        carry = lax.fori_loop(0, qi, functools.partial(tile, masked=False), init)
        m, l, acc = tile(qi, carry, True)
        outs.append((in_head, acc / l))
    o_ref[...] = jnp.where(outs[0][0], outs[0][1], outs[1][1])


def _attn_prompt(q_bf, k_bf, v_bf, c_rows, c_t, *, batch, seq, tq):
    rows, w = q_bf.shape
    nq = seq // tq
    return pl.pallas_call(
        functools.partial(_attn_prompt_kernel, tq=tq),
        grid=(batch, w // LANES, nq),
        in_specs=[pl.BlockSpec((tq, LANES), lambda b, hp, qi: (b * nq + qi, hp)),
                  pl.BlockSpec((seq, LANES), lambda b, hp, qi: (b, hp)),
                  pl.BlockSpec((seq, LANES), lambda b, hp, qi: (b, hp)),
                  pl.BlockSpec((tq, LANES), lambda b, hp, qi: (b * nq + qi, 0)),
                  pl.BlockSpec((SUBLANES, seq), lambda b, hp, qi: (0, b))],
        out_specs=pl.BlockSpec((tq, LANES), lambda b, hp, qi: (b * nq + qi, hp)),
        out_shape=jax.ShapeDtypeStruct((rows, w), F32),
        compiler_params=pltpu.CompilerParams(
            dimension_semantics=("arbitrary", "arbitrary", "arbitrary"),
            vmem_limit_bytes=VMEM_LIMIT),
        name="attn_prompt",
    )(q_bf, k_bf, v_bf, c_rows, c_t)


def _attn_decode_kernel(pt_ref, q_ref, kn_ref, vn_ref, lfn_ref, *rest, n_pg, nt):
    k_refs = rest[:n_pg]
    v_refs = rest[n_pg:2 * n_pg]
    lf_refs = rest[2 * n_pg:3 * n_pg]
    (o_ref, qe_ref, cn_ref, cnl_ref, m_ref, l_ref, acc_ref, carry_ref, knp_ref,
     vnp_ref) = rest[3 * n_pg:]
    j = pl.program_id(1)
    w = q_ref.shape[2]
    n_keys = n_pg * LANES
    rows = nt * HEADS
    lane_p = lax.broadcasted_iota(jnp.int32, (HEADS, LANES), 1)
    sub = lax.broadcasted_iota(jnp.int32, (HEADS, w), 0)
    lane_w = lax.broadcasted_iota(jnp.int32, (HEADS, w), 1)
    own_lanes = (lane_w >= sub * HEAD_DIM) & (lane_w < (sub + 1) * HEAD_DIM)

    @pl.when(j == 0)
    def _():
        q = q_ref[0]
        lfn = lfn_ref[0]
        cn = jnp.zeros((HEADS, 1), F32)
        cn_lanes = jnp.zeros((HEADS, LANES), F32)
        for t in range(nt):
            qe_ref[t * HEADS:(t + 1) * HEADS, :] = jnp.where(
                own_lanes, jnp.broadcast_to(q[t:t + 1, :], (HEADS, w)), 0.0)
            cn = cn + jnp.sum(jnp.where(lane_p == t, lfn, 0.0), axis=1, keepdims=True)
            cn_ref[t * HEADS:(t + 1) * HEADS, :] = cn
            cn_lanes = jnp.where(lane_p == t, cn, cn_lanes)
        cnl_ref[...] = cn_lanes
        m_ref[...] = jnp.full(m_ref.shape, NEG_INF, F32)
        l_ref[...] = jnp.zeros(l_ref.shape, F32)
        acc_ref[...] = jnp.zeros(acc_ref.shape, F32)
        carry_ref[...] = jnp.zeros(carry_ref.shape, F32)
        knp_ref[...] = jnp.zeros(knp_ref.shape, F32)
        vnp_ref[...] = jnp.zeros(vnp_ref.shape, F32)
        knp_ref[0:nt, :] = kn_ref[0]
        vnp_ref[0:nt, :] = vn_ref[0]

    qe = qe_ref[...].astype(BF16)
    cn = cn_ref[...]

    def online_update(z, v_bf):
        m_old = m_ref[...]
        m_new = jnp.maximum(m_old, jnp.max(z, axis=-1, keepdims=True) + cn)
        p = jnp.exp(z - (m_new - cn))
        alpha = jnp.exp(m_old - m_new)
        l_ref[...] = alpha * l_ref[...] + jnp.sum(p, axis=-1, keepdims=True)
        acc_ref[...] = alpha * acc_ref[...] + _dot(p.astype(BF16), v_bf)
        m_ref[...] = m_new

    lf = jnp.concatenate([r[0] for r in lf_refs], axis=1)
    lane_k = lax.broadcasted_iota(jnp.int32, lf.shape, 1)
    suf = lf
    d = 1
    while d < n_keys:
        suf = suf + jnp.where(lane_k < n_keys - d, pltpu.roll(suf, n_keys - d, axis=1), 0.0)
        d *= 2
    carry = carry_ref[:, 0:1]
    rc = (suf - lf) + carry
    carry_ref[...] = jnp.broadcast_to(carry + suf[:, 0:1], carry_ref.shape)
    k_bf = jnp.concatenate([r[0].astype(BF16) for r in k_refs], axis=0)
    v_bf = jnp.concatenate([r[0].astype(BF16) for r in v_refs], axis=0)
    z = _dot_nt(qe, k_bf) + jnp.concatenate([rc] * nt, axis=0)
    online_update(z, v_bf)

    @pl.when(j == pl.num_programs(1) - 1)
    def _():
        z_new = _dot_nt(qe, knp_ref[...].astype(BF16)) - jnp.concatenate([cnl_ref[...]] * nt, axis=0)
        r_id = lax.broadcasted_iota(jnp.int32, (rows, LANES), 0)
        c_id = lax.broadcasted_iota(jnp.int32, (rows, LANES), 1)
        z_new = jnp.where(c_id * HEADS <= r_id, z_new, NEG_INF)
        online_update(z_new, vnp_ref[...].astype(BF16))
        o = acc_ref[...] / l_ref[...]
        for t in range(nt):
            o_t = jnp.where(own_lanes, o[t * HEADS:(t + 1) * HEADS, :], 0.0)
            o_ref[0, t:t + 1, :] = jnp.sum(o_t, axis=0, keepdims=True)


def _attn_decode(page_table, q, k_new, v_new, lf_new_t, cache_k, cache_v, cache_lf_t, *, n_pg):
    nb, nt, w = q.shape
    n_pages = page_table.shape[1]
    n_chunks = n_pages // n_pg
    page = cache_k.shape[1]
    rows = nt * HEADS

    def seq_spec(shape):
        return pl.BlockSpec((1,) + shape, lambda b, j, pt: (b, 0, 0))

    def page_spec(shape, i):
        return pl.BlockSpec((1,) + shape, lambda b, j, pt: (pt[b, (n_chunks - 1 - j) * n_pg + i], 0, 0))

    in_specs = ([seq_spec((nt, w)) for _ in range(3)] + [seq_spec((HEADS, LANES))]
                + [page_spec((page, w), i) for i in range(n_pg)]
                + [page_spec((page, w), i) for i in range(n_pg)]
                + [page_spec((HEADS, page), i) for i in range(n_pg)])
    grid_spec = pltpu.PrefetchScalarGridSpec(
        num_scalar_prefetch=1, grid=(nb, n_chunks), in_specs=in_specs,
        out_specs=seq_spec((nt, w)),
        scratch_shapes=[pltpu.VMEM((rows, w), F32), pltpu.VMEM((rows, 1), F32),
                        pltpu.VMEM((HEADS, LANES), F32),
                        pltpu.VMEM((rows, 1), F32), pltpu.VMEM((rows, 1), F32),
                        pltpu.VMEM((rows, w), F32), pltpu.VMEM((HEADS, LANES), F32),
                        pltpu.VMEM((LANES, w), F32), pltpu.VMEM((LANES, w), F32)])
    return pl.pallas_call(
        functools.partial(_attn_decode_kernel, n_pg=n_pg, nt=nt),
        grid_spec=grid_spec,
        out_shape=jax.ShapeDtypeStruct((nb, nt, w), F32),
        compiler_params=pltpu.CompilerParams(dimension_semantics=("arbitrary", "arbitrary"),
                                             vmem_limit_bytes=VMEM_LIMIT),
        name="attn_decode",
    )(page_table, q, k_new, v_new, lf_new_t, *([cache_k] * n_pg), *([cache_v] * n_pg),
      *([cache_lf_t] * n_pg))


def _ffn_math(x, ylru_bf, yfox, p, wr, ext_ref, fill_prev, keep_tail, *, pad, unit):
    (gfox_ref, wout_ref, gffn_ref, wup_ref, cw_ref, cb_ref, wdown_ref, wple_ref, wgate_ref,
     gfin_ref) = wr
    n = x.shape[0]
    wl = ylru_bf.shape[1]
    yf = _rms(yfox, gfox_ref[...]).astype(BF16)
    h = x + _dot(ylru_bf, wout_ref[0:wl, :]) + _dot(yf, wout_ref[wl:, :])
    n2 = _rms(h, gffn_ref[...]).astype(BF16)
    d_ff = wdown_ref.shape[0]
    cw = ext_ref.shape[1]
    acc = jnp.zeros(x.shape, F32)
    for c in range(d_ff // cw):
        cs = slice(c * cw, (c + 1) * cw)
        g = _dot(n2, wup_ref[:, cs])
        u = _dot(n2, wup_ref[:, d_ff + c * cw:d_ff + (c + 1) * cw])
        fill_prev(cs)
        ext_ref[pad:pad + n, :] = g
        gc = _causal_conv(ext_ref, cw_ref, cb_ref, cs, pad=pad, unit=unit, n=n)
        keep_tail(cs)
        acc = acc + _dot((_gelu(gc) * u).astype(BF16), wdown_ref[cs, :])
    h = h + acc
    e = _dot(p.astype(BF16), wple_ref[...]) * jax.nn.sigmoid(_dot(h.astype(BF16), wgate_ref[...]))
    return _rms(h + e, gfin_ref[...])


def _ffn_prompt_kernel(x_ref, ylru_ref, yfox_ref, p_ref, *rest, tm):
    wr = rest[:10]
    y_ref, tail_ref, ext_ref, carry_ref = rest[10:]
    t = pl.program_id(1)
    pad = SUBLANES
    taps = wr[4].shape[0]

    @pl.when(t == 0)
    def _():
        carry_ref[...] = jnp.zeros(carry_ref.shape, F32)

    def fill_prev(cs):
        ext_ref[0:pad, :] = carry_ref[:, cs]

    def keep_tail(cs):
        carry_ref[:, cs] = ext_ref[tm:tm + pad, :]
        tail_ref[0, :, cs] = ext_ref[pad + tm - (taps - 1):pad + tm, :]

    y_ref[...] = _ffn_math(x_ref[...], ylru_ref[...], yfox_ref[...], p_ref[...], wr, ext_ref,
                           fill_prev, keep_tail, pad=pad, unit=1)


def _ffn_prompt(x, ylru, yfox, p, weights, *, batch, seq, tm, conv_cols):
    rows, d = x.shape
    nt = seq // tm
    d_ff = weights[6].shape[0]
    taps = weights[4].shape[0]
    tile = lambda c: pl.BlockSpec((tm, c), lambda b, t: (b * nt + t, 0))
    return pl.pallas_call(
        functools.partial(_ffn_prompt_kernel, tm=tm),
        grid=(batch, nt),
        in_specs=[tile(d), tile(ylru.shape[1]), tile(yfox.shape[1]), tile(p.shape[1])]
        + [_const_spec(wt.shape) for wt in weights],
        out_specs=[tile(d), pl.BlockSpec((1, taps - 1, d_ff), lambda b, t: (b, 0, 0))],
        out_shape=[jax.ShapeDtypeStruct((rows, d), F32),
                   jax.ShapeDtypeStruct((batch, taps - 1, d_ff), F32)],
        scratch_shapes=[pltpu.VMEM((SUBLANES + tm, conv_cols), F32),
                        pltpu.VMEM((SUBLANES, d_ff), F32)],
        compiler_params=pltpu.CompilerParams(dimension_semantics=("arbitrary", "arbitrary"),
                                             vmem_limit_bytes=VMEM_LIMIT),
        name="ffn_prompt",
    )(x, ylru, yfox, p, *weights)


def _ffn_sample_kernel(x_ref, ylru_ref, yfox_ref, p_ref, st_ref, *rest, nt):
    wr = rest[:10]
    y_ref, tail_ref, ext_ref = rest[10:]
    nb = x_ref.shape[0]
    d = x_ref.shape[1] // nt
    d_ff = wr[6].shape[0]
    taps = wr[4].shape[0]
    pad = (taps - 1) * nb

    def fill_prev(cs):
        for j in range(taps - 1):
            ext_ref[j * nb:(j + 1) * nb, :] = st_ref[:, j * d_ff + cs.start:j * d_ff + cs.stop]

    def keep_tail(cs):
        for j in range(taps - 1):
            tail_ref[:, j * d_ff + cs.start:j * d_ff + cs.stop] = ext_ref[(nt + j) * nb:(nt + j + 1) * nb, :]

    y = _ffn_math(_time_major(x_ref, nt), _time_major(ylru_ref, nt), _time_major(yfox_ref, nt),
                  _time_major(p_ref, nt), wr, ext_ref, fill_prev, keep_tail, pad=pad, unit=nb)
    for t in range(nt):
        y_ref[:, t * d:(t + 1) * d] = y[t * nb:(t + 1) * nb, :]


def _ffn_sample(x, ylru, yfox, p, state, weights, *, nt, conv_cols):
    nb = x.shape[0]
    taps = weights[4].shape[0]
    args = [x, ylru, yfox, p, state] + list(weights)
    return pl.pallas_call(
        functools.partial(_ffn_sample_kernel, nt=nt),
        grid=(1,),
        in_specs=[_const_spec(a.shape) for a in args],
        out_specs=[_whole_spec(x.shape), _whole_spec(state.shape)],
        out_shape=[jax.ShapeDtypeStruct(x.shape, F32), jax.ShapeDtypeStruct(state.shape, F32)],
        scratch_shapes=[pltpu.VMEM(((taps - 1 + nt) * nb, conv_cols), F32)],
        compiler_params=pltpu.CompilerParams(dimension_semantics=("arbitrary",),
                                             vmem_limit_bytes=VMEM_LIMIT),
        name="ffn_sample",
    )(*args)


def _block_diag(w):
    nblk, k, j = w.shape
    eye = jnp.eye(nblk, dtype=w.dtype)
    return (w[:, :, None, :] * eye[:, None, :, None]).reshape(nblk * k, nblk * j)


def kernel(x_prompt, x_sample, p_prompt, p_sample, cache_k, cache_v, cache_logf, state_lru_h, state_lru_conv, state_ffn_conv, page_table, g_mix, w_in, lru_conv_w, lru_conv_b, lru_w_a, lru_b_a, lru_w_x, lru_b_x, lru_lambda, fox_b_f, g_lru_out, g_fox_out, w_out, g_ffn, w_up, ffn_conv_w, ffn_conv_b, w_down, w_ple, w_ple_gate, g_final):
    depth = w_in.shape[0]
    assert depth == 1, "single-layer step"
    bp, sp, d = x_prompt.shape
    bs, ts, _ = x_sample.shape
    lw = state_lru_h.shape[-1]
    fw = HEADS * HEAD_DIM
    d_ff = w_down.shape[1]
    n_phys, page = cache_k.shape[1], cache_k.shape[2]
    l = 0
    row = lambda a: a.reshape(1, -1)

    w_main = w_in[l][:, :2 * lw + 3 * fw].astype(BF16)
    w_f = jnp.pad(w_in[l][:, 2 * lw + 3 * fw:], ((0, 0), (0, LANES - HEADS))).astype(BF16)
    b_f = jnp.pad(fox_b_f[l], (0, LANES - HEADS)).reshape(1, LANES)
    lru_w = [lru_conv_w[l], row(lru_conv_b[l]), _block_diag(lru_w_a[l]).astype(BF16), row(lru_b_a[l]),
             _block_diag(lru_w_x[l]).astype(BF16), row(lru_b_x[l]), row(lru_lambda[l]), row(g_lru_out[l])]
    ffn_w = [row(g_fox_out[l]), w_out[l].astype(BF16), row(g_ffn[l]), w_up[l].astype(BF16),
             ffn_conv_w[l], row(ffn_conv_b[l]), w_down[l].astype(BF16), w_ple[l].astype(BF16),
             w_ple_gate[l].astype(BF16), row(g_final)]
    g_mix_r = row(g_mix[l])

    xp = x_prompt.reshape(bp * sp, d)
    xr, gr, q_bf, k_p, v_p, lf_pad, k_bf, v_bf = _inproj(
        xp, g_mix_r, w_main, w_f, b_f, tm=512, nt=1, q_dtype=BF16, emit_bf16_kv=True)
    logf_p, c_rows, c_t = _cumsum(lf_pad, bp, sp)
    ylru_p, hlast_p = _lru_prompt(xr, gr, *lru_w, batch=bp, seq=sp, tt=512)
    yfox_p = _attn_prompt(q_bf, k_bf, v_bf, c_rows, c_t, batch=bp, seq=sp, tq=512)
    y_p, ffn_tail_p = _ffn_prompt(xp, ylru_p, yfox_p, p_prompt[l].reshape(bp * sp, -1), ffn_w,
                                  batch=bp, seq=sp, tm=512, conv_cols=512)

    xs = x_sample.reshape(bs, ts * d)
    xr_s, gr_s, q_s, k_s, v_s, lf_s = _inproj(
        xs, g_mix_r, w_main, w_f, b_f, tm=bs, nt=ts, q_dtype=F32, emit_bf16_kv=False)
    logf_s = lf_s.reshape(bs, ts, LANES)[:, :, :HEADS]
    ylru_s, hlast_s = _lru_sample(
        xr_s, gr_s, state_lru_h[l], state_lru_conv[l].reshape(bs, -1), *lru_w, nt=ts)
    yfox_s = _attn_decode(
        page_table, q_s.reshape(bs, ts, fw), k_s.reshape(bs, ts, fw), v_s.reshape(bs, ts, fw),
        jnp.pad(jnp.swapaxes(logf_s, 1, 2), ((0, 0), (0, 0), (0, LANES - ts))),
        cache_k[l].reshape(n_phys, page, fw), cache_v[l].reshape(n_phys, page, fw),
        jnp.swapaxes(cache_logf[l], 1, 2), n_pg=16)
    y_s, ffn_tail_s = _ffn_sample(
        xs, ylru_s, yfox_s.reshape(bs, ts * fw), p_sample[l].reshape(bs, -1),
        state_ffn_conv[l].reshape(bs, -1), ffn_w, nt=ts, conv_cols=512)

    lru_taps = lru_conv_w.shape[1]
    return (y_p.reshape(bp, sp, d), y_s.reshape(bs, ts, d),
            k_p.reshape(1, bp, sp, HEADS, HEAD_DIM), v_p.reshape(1, bp, sp, HEADS, HEAD_DIM),
            logf_p.reshape(1, bp, sp, HEADS), hlast_p.reshape(1, bp, lw),
            xr.reshape(bp, sp, lw)[None, :, sp - (lru_taps - 1):, :], ffn_tail_p[None],
            k_s.reshape(1, bs, ts, HEADS, HEAD_DIM), v_s.reshape(1, bs, ts, HEADS, HEAD_DIM),
            logf_s[None], hlast_s[None],
            xr_s.reshape(bs, ts, lw)[None, :, ts - (lru_taps - 1):, :],
            ffn_tail_s.reshape(1, bs, -1, d_ff))
```

```python
import functools
import math

import jax
import jax.numpy as jnp
from jax import lax
from jax.experimental import pallas as pl
from jax.experimental.pallas import tpu as pltpu

F32 = jnp.float32
BF16 = jnp.bfloat16

EPS = 1e-6
LRU_C = 8.0
HEAD_DIM = 64
HEADS = 8
LANES = 128
SUBLANES = 8
VMEM_LIMIT = 56 * 1024 * 1024
NEG_INF = float("-inf")
GELU_C = math.sqrt(2.0 / math.pi)
LOG2_E = math.log2(math.e)


def _rms(x, g):
    ms = jnp.mean(x * x, axis=-1, keepdims=True)
    return x * lax.rsqrt(ms + EPS) * g


def _gelu(x):
    return x * (0.5 * (1.0 + jnp.tanh(GELU_C * (x + 0.044715 * (x * x * x)))))


def _softplus(x):
    return jnp.maximum(x, 0.0) + jnp.log1p(jnp.exp(-jnp.abs(x)))


def _dot(a, b):
    return jnp.dot(a, b, preferred_element_type=F32)


def _dot_nt(a, b):
    return lax.dot_general(a, b, (((1,), (1,)), ((), ())), preferred_element_type=F32)


def _const_spec(shape):
    nd = len(shape)
    return pl.BlockSpec(shape, lambda *_: (0,) * nd, pipeline_mode=pl.Buffered(1))


def _whole_spec(shape):
    nd = len(shape)
    return pl.BlockSpec(shape, lambda *_: (0,) * nd)


def _time_major(ref, nt):
    c = ref.shape[1] // nt
    return jnp.concatenate([ref[:, t * c:(t + 1) * c] for t in range(nt)], axis=0)


def _inproj_sample_kernel(x_ref, g_ref, wm_ref, wf_ref, bf_ref,
                          xr_ref, gr_ref, q_ref, k_ref, v_ref, lf_ref, *, nt):
    d = x_ref.shape[1] // nt
    w = xr_ref.shape[1] // nt
    for t in range(nt):
        n = _rms(x_ref[:, t * d:(t + 1) * d], g_ref[...]).astype(BF16)
        ws = slice(t * w, (t + 1) * w)
        xr_ref[:, ws] = _dot(n, wm_ref[:, 0 * w:1 * w])
        gr_ref[:, ws] = _dot(n, wm_ref[:, 1 * w:2 * w])
        q_ref[:, ws] = _dot(n, wm_ref[:, 2 * w:3 * w]) * (HEAD_DIM ** -0.5)
        k_ref[:, ws] = _dot(n, wm_ref[:, 3 * w:4 * w])
        v_ref[:, ws] = _dot(n, wm_ref[:, 4 * w:5 * w])
        fl = _dot(n, wf_ref[...]) + bf_ref[...]
        lf_ref[:, t * LANES:(t + 1) * LANES] = -_softplus(-fl)


def _inproj_sample(x, g_mix, w_main, w_f, b_f, *, nt):
    rows = x.shape[0]
    w = w_main.shape[1] // 5
    args = [x, g_mix, w_main, w_f, b_f]
    shapes = [(rows, nt * w)] * 5 + [(rows, nt * LANES)]
    return pl.pallas_call(
        functools.partial(_inproj_sample_kernel, nt=nt),
        grid=(1,),
        in_specs=[_const_spec(a.shape) for a in args],
        out_specs=[_whole_spec(s) for s in shapes],
        out_shape=[jax.ShapeDtypeStruct(s, F32) for s in shapes],
        compiler_params=pltpu.CompilerParams(dimension_semantics=("arbitrary",),
                                             vmem_limit_bytes=VMEM_LIMIT),
        name="inproj_sample",
    )(*args)


def _inproj_prompt_kernel(x_ref, g_ref, wm_ref, wkvt_ref, wf_ref, bf_ref,
                          xr_ref, gr_ref, q_ref, kt_ref, vt_ref, ktb_ref, vtb_ref, lf_ref):
    n = _rms(x_ref[...], g_ref[...]).astype(BF16)
    w = xr_ref.shape[1]
    xr_ref[...] = _dot(n, wm_ref[:, 0 * w:1 * w])
    gr_ref[...] = _dot(n, wm_ref[:, 1 * w:2 * w])
    q_ref[...] = (_dot(n, wm_ref[:, 2 * w:3 * w]) * (HEAD_DIM ** -0.5 * LOG2_E)).astype(q_ref.dtype)
    kt = _dot_nt(wkvt_ref[0:w, :], n)
    vt = _dot_nt(wkvt_ref[w:2 * w, :], n)
    kt_ref[...] = kt
    vt_ref[...] = vt
    ktb_ref[...] = kt.astype(BF16)
    vtb_ref[...] = vt.astype(BF16)
    fl = _dot(n, wf_ref[...]) + bf_ref[...]
    lf_ref[...] = -_softplus(-fl)


def _inproj_prompt(x, g_mix, w_main, w_kvt, w_f, b_f, *, batch, seq, tm):
    rows, d = x.shape
    w = w_kvt.shape[0] // 2
    nt = seq // tm
    row_spec = lambda c: pl.BlockSpec((tm, c), lambda b, t: (b * nt + t, 0))
    col_spec = pl.BlockSpec((w, tm), lambda b, t: (b, t))
    consts = [g_mix, w_main, w_kvt, w_f, b_f]
    return pl.pallas_call(
        _inproj_prompt_kernel,
        grid=(batch, nt),
        in_specs=[row_spec(d)] + [_const_spec(c.shape) for c in consts],
        out_specs=[row_spec(w), row_spec(w), row_spec(w), col_spec, col_spec, col_spec, col_spec,
                   row_spec(LANES)],
        out_shape=[jax.ShapeDtypeStruct((rows, w), F32), jax.ShapeDtypeStruct((rows, w), F32),
                   jax.ShapeDtypeStruct((rows, w), BF16),
                   jax.ShapeDtypeStruct((batch * w, seq), F32), jax.ShapeDtypeStruct((batch * w, seq), F32),
                   jax.ShapeDtypeStruct((batch * w, seq), BF16), jax.ShapeDtypeStruct((batch * w, seq), BF16),
                   jax.ShapeDtypeStruct((rows, LANES), F32)],
        compiler_params=pltpu.CompilerParams(dimension_semantics=("arbitrary", "arbitrary"),
                                             vmem_limit_bytes=VMEM_LIMIT),
        name="inproj_prompt",
    )(x, *consts)


def _cumsum_kernel(lf_ref, lft_ref, c_ref, ct_ref):
    x = lf_ref[...]
    lft_ref[0] = x.T[:HEADS, :]
    n = x.shape[0]
    row = lax.broadcasted_iota(jnp.int32, x.shape, 0)
    d = 1
    while d < n:
        x = x + jnp.where(row >= d, pltpu.roll(x, d, axis=0), 0.0)
        d *= 2
    x = x * LOG2_E
    c_ref[...] = x
    ct_ref[0] = x.T[:HEADS, :]


def _cumsum(lf_pad, batch, seq):
    rows = lf_pad.shape[0]
    head_major = pl.BlockSpec((1, HEADS, seq), lambda b: (b, 0, 0))
    return pl.pallas_call(
        _cumsum_kernel,
        grid=(batch,),
        in_specs=[pl.BlockSpec((seq, LANES), lambda b: (b, 0))],
        out_specs=[head_major, pl.BlockSpec((seq, LANES), lambda b: (b, 0)), head_major],
        out_shape=[jax.ShapeDtypeStruct((batch, HEADS, seq), F32),
                   jax.ShapeDtypeStruct((rows, LANES), F32),
                   jax.ShapeDtypeStruct((batch, HEADS, seq), F32)],
        compiler_params=pltpu.CompilerParams(dimension_semantics=("arbitrary",),
                                             vmem_limit_bytes=VMEM_LIMIT),
        name="logf_cumsum",
    )(lf_pad)


def _causal_conv(ext_ref, cw_ref, cb_ref, cols, *, pad, unit, n):
    taps = cw_ref.shape[0]
    out = cb_ref[:, cols]
    for j in range(taps):
        out = out + cw_ref[taps - 1 - j:taps - j, cols] * ext_ref[pl.ds(pad - j * unit, n), :]
    return out


def _lru_gates(xc, wa_ref, ba_ref, wx_ref, bx_ref, lam_ref):
    xcb = xc.astype(BF16)
    r = jax.nn.sigmoid(_dot(xcb, wa_ref[...]) + ba_ref[...])
    i = jax.nn.sigmoid(_dot(xcb, wx_ref[...]) + bx_ref[...])
    log_a = (-LRU_C) * r * _softplus(-lam_ref[...])
    a = jnp.exp(log_a)
    mult = jnp.sqrt(-jnp.tanh(log_a) * (a * a + 1.0))
    return a, mult, i * xc


def _lru_prompt_kernel(xr_ref, gr_ref, cw_ref, cb_ref, wa_ref, ba_ref, wx_ref, bx_ref, lam_ref,
                       g_ref, y_ref, hlast_ref, ext_ref, a_ref, u_ref, hs_ref, h_ref, *, tt):
    t = pl.program_id(1)
    pad = SUBLANES

    @pl.when(t == 0)
    def _():
        ext_ref[0:pad, :] = jnp.zeros((pad, ext_ref.shape[1]), F32)
        h_ref[...] = jnp.zeros(h_ref.shape, F32)

    x = xr_ref[...]
    ext_ref[pad:pad + tt, :] = x
    xc = _causal_conv(ext_ref, cw_ref, cb_ref, slice(None), pad=pad, unit=1, n=tt)
    ext_ref[0:pad, :] = x[tt - pad:tt, :]

    a, mult, ix = _lru_gates(xc, wa_ref, ba_ref, wx_ref, bx_ref, lam_ref)
    row = lax.broadcasted_iota(jnp.int32, (tt, 1), 0)
    mult = jnp.where((row == 0) & (t == 0), 1.0, mult)
    a_ref[...] = a
    u_ref[...] = mult * ix

    row8 = lax.broadcasted_iota(jnp.int32, (SUBLANES, a_ref.shape[1]), 0)

    def group(gi, h):
        r0 = pl.multiple_of(gi * SUBLANES, SUBLANES)
        ag = a_ref[pl.ds(r0, SUBLANES), :]
        ug = u_ref[pl.ds(r0, SUBLANES), :]
        for d in (1, 2, 4):
            a_sh = jnp.where(row8 >= d, pltpu.roll(ag, d, axis=0), 1.0)
            u_sh = jnp.where(row8 >= d, pltpu.roll(ug, d, axis=0), 0.0)
            ug = ag * u_sh + ug
            ag = ag * a_sh
        hs = ag * h + ug
        hs_ref[pl.ds(r0, SUBLANES), :] = hs
        return jnp.broadcast_to(hs[SUBLANES - 1:SUBLANES, :], hs.shape)

    h = lax.fori_loop(0, tt // SUBLANES, group, h_ref[...], unroll=2)
    h_ref[...] = h
    hlast_ref[0] = h[0:1, :]
    y = hs_ref[...] * _gelu(gr_ref[...])
    y_ref[...] = _rms(y, g_ref[...]).astype(y_ref.dtype)


def _lru_prompt(xr, gr, cw, cb, wa, ba, wx, bx, lam, g_out, *, batch, seq, tt):
    rows, w = xr.shape
    nt = seq // tt
    tile = pl.BlockSpec((tt, w), lambda b, t: (b * nt + t, 0))
    consts = [cw, cb, wa, ba, wx, bx, lam, g_out]
    return pl.pallas_call(
        functools.partial(_lru_prompt_kernel, tt=tt),
        grid=(batch, nt),
        in_specs=[tile, tile] + [_const_spec(c.shape) for c in consts],
        out_specs=[tile, pl.BlockSpec((1, 1, w), lambda b, t: (b, 0, 0))],
        out_shape=[jax.ShapeDtypeStruct((rows, w), BF16), jax.ShapeDtypeStruct((batch, 1, w), F32)],
        scratch_shapes=[pltpu.VMEM((SUBLANES + tt, w), F32), pltpu.VMEM((tt, w), F32),
                        pltpu.VMEM((tt, w), F32), pltpu.VMEM((tt, w), F32),
                        pltpu.VMEM((SUBLANES, w), F32)],
        compiler_params=pltpu.CompilerParams(dimension_semantics=("arbitrary", "arbitrary"),
                                             vmem_limit_bytes=VMEM_LIMIT),
        name="lru_prompt",
    )(xr, gr, *consts)


def _lru_sample_kernel(xr_ref, gr_ref, h0_ref, cbuf_ref, cw_ref, cb_ref, wa_ref, ba_ref, wx_ref,
                       bx_ref, lam_ref, g_ref, y_ref, hlast_ref, ext_ref, *, nt):
    nb, w = h0_ref.shape
    taps = cw_ref.shape[0]
    pad = (taps - 1) * nb
    n = nb * nt
    ext_ref[0:pad, :] = _time_major(cbuf_ref, taps - 1)
    ext_ref[pad:pad + n, :] = _time_major(xr_ref, nt)
    xc = _causal_conv(ext_ref, cw_ref, cb_ref, slice(None), pad=pad, unit=nb, n=n)
    a, mult, ix = _lru_gates(xc, wa_ref, ba_ref, wx_ref, bx_ref, lam_ref)
    u = mult * ix
    h = h0_ref[...]
    for t in range(nt):
        h = a[t * nb:(t + 1) * nb, :] * h + u[t * nb:(t + 1) * nb, :]
        y = h * _gelu(gr_ref[:, t * w:(t + 1) * w])
        y_ref[:, t * w:(t + 1) * w] = _rms(y, g_ref[...]).astype(y_ref.dtype)
    hlast_ref[...] = h


def _lru_sample(xr, gr, h0, cbuf, cw, cb, wa, ba, wx, bx, lam, g_out, *, nt):
    nb, w = h0.shape
    args = [xr, gr, h0, cbuf, cw, cb, wa, ba, wx, bx, lam, g_out]
    taps = cw.shape[0]
    return pl.pallas_call(
        functools.partial(_lru_sample_kernel, nt=nt),
        grid=(1,),
        in_specs=[_const_spec(a.shape) for a in args],
        out_specs=[_whole_spec((nb, nt * w)), _whole_spec((nb, w))],
        out_shape=[jax.ShapeDtypeStruct((nb, nt * w), BF16), jax.ShapeDtypeStruct((nb, w), F32)],
        scratch_shapes=[pltpu.VMEM(((taps - 1 + nt) * nb, w), F32)],
        compiler_params=pltpu.CompilerParams(dimension_semantics=("arbitrary",),
                                             vmem_limit_bytes=VMEM_LIMIT),
        name="lru_sample",
    )(*args)


def _attn_prompt_kernel(q_ref, kt_ref, vt_ref, c_ref, ct_ref, o_ref, ck_ref, *, tq, tk):
    hp = pl.program_id(1)
    qi = pl.program_id(2)
    heads_per_blk = LANES // HEAD_DIM
    q = q_ref[...]
    c_blk = c_ref[...]
    lane = lax.broadcasted_iota(jnp.int32, (tq, LANES), 1)
    n_full = (qi * tq) // tk
    n_diag = pl.cdiv(tq, tk)
    outs = []
    for hh in range(heads_per_blk):
        h = hp * heads_per_blk + hh
        in_head = (lane >= hh * HEAD_DIM) & (lane < (hh + 1) * HEAD_DIM)
        qm = jnp.where(in_head, q, jnp.zeros_like(q))
        cq = jnp.sum(jnp.where(lane == h, c_blk, 0.0), axis=-1, keepdims=True)
        ck_ref[...] = ct_ref[pl.ds(h, 1), :]

        def tile(j, carry, masked):
            m, l, acc = carry
            ks = pl.multiple_of(j * tk, tk)
            z = _dot(qm, kt_ref[:, pl.ds(ks, tk)]) - ck_ref[:, pl.ds(ks, tk)]
            if masked:
                row_i = lax.broadcasted_iota(jnp.int32, (tq, tk), 0) + qi * tq
                col_i = lax.broadcasted_iota(jnp.int32, (tq, tk), 1) + j * tk
                z = jnp.where(col_i <= row_i, z, NEG_INF)
            m_new = jnp.maximum(m, jnp.max(z, axis=-1, keepdims=True) + cq)
            p = jnp.exp2(z - (m_new - cq))
            alpha = jnp.exp2(m - m_new)
            l = alpha * l + jnp.sum(p, axis=-1, keepdims=True)
            acc = alpha * acc + _dot_nt(p.astype(BF16), vt_ref[:, pl.ds(ks, tk)])
            return m_new, l, acc

        carry = (jnp.full((tq, 1), NEG_INF, F32), jnp.zeros((tq, 1), F32), jnp.zeros((tq, LANES), F32))
        carry = lax.fori_loop(0, n_full, functools.partial(tile, masked=False), carry)
        for dj in range(n_diag):
            carry = tile(n_full + dj, carry, True)
        m, l, acc = carry
        outs.append((in_head, acc / l))
    o_ref[...] = jnp.where(outs[0][0], outs[0][1], outs[1][1])


def _attn_prompt(q_bf, kt_bf, vt_bf, c_rows, c_t, *, batch, seq, tq, tk):
    rows, w = q_bf.shape
    nq = seq // tq
    n_hp = w // LANES
    kv_spec = pl.BlockSpec((LANES, seq), lambda b, hp, qi: (b * n_hp + hp, 0))
    return pl.pallas_call(
        functools.partial(_attn_prompt_kernel, tq=tq, tk=tk),
        grid=(batch, n_hp, nq),
        in_specs=[pl.BlockSpec((tq, LANES), lambda b, hp, qi: (b * nq + qi, hp)),
                  kv_spec, kv_spec,
                  pl.BlockSpec((tq, LANES), lambda b, hp, qi: (b * nq + qi, 0)),
                  pl.BlockSpec((HEADS, seq), lambda b, hp, qi: (b, 0))],
        out_specs=pl.BlockSpec((tq, LANES), lambda b, hp, qi: (b * nq + qi, hp)),
        out_shape=jax.ShapeDtypeStruct((rows, w), F32),
        scratch_shapes=[pltpu.VMEM((1, seq), F32)],
        compiler_params=pltpu.CompilerParams(
            dimension_semantics=("arbitrary", "arbitrary", "arbitrary"),
            vmem_limit_bytes=VMEM_LIMIT),
        name="attn_prompt",
    )(q_bf, kt_bf, vt_bf, c_rows, c_t)


def _attn_decode_kernel(pt_ref, q_ref, kn_ref, vn_ref, lfn_ref, *rest, n_pg, nt):
    k_refs = rest[:n_pg]
    v_refs = rest[n_pg:2 * n_pg]
    lf_refs = rest[2 * n_pg:3 * n_pg]
    (o_ref, qe_ref, cn_ref, cnl_ref, m_ref, l_ref, acc_ref, carry_ref, knp_ref,
     vnp_ref) = rest[3 * n_pg:]
    j = pl.program_id(1)
    w = q_ref.shape[2]
    n_keys = n_pg * LANES
    rows = nt * HEADS
    lane_p = lax.broadcasted_iota(jnp.int32, (HEADS, LANES), 1)
    sub = lax.broadcasted_iota(jnp.int32, (HEADS, w), 0)
    lane_w = lax.broadcasted_iota(jnp.int32, (HEADS, w), 1)
    own_lanes = (lane_w >= sub * HEAD_DIM) & (lane_w < (sub + 1) * HEAD_DIM)

    @pl.when(j == 0)
    def _():
        q = q_ref[0]
        lfn = lfn_ref[0]
        cn = jnp.zeros((HEADS, 1), F32)
        cn_lanes = jnp.zeros((HEADS, LANES), F32)
        for t in range(nt):
            qe_ref[t * HEADS:(t + 1) * HEADS, :] = jnp.where(
                own_lanes, jnp.broadcast_to(q[t:t + 1, :], (HEADS, w)), 0.0)
            cn = cn + jnp.sum(jnp.where(lane_p == t, lfn, 0.0), axis=1, keepdims=True)
            cn_ref[t * HEADS:(t + 1) * HEADS, :] = cn
            cn_lanes = jnp.where(lane_p == t, cn, cn_lanes)
        cnl_ref[...] = cn_lanes
        m_ref[...] = jnp.full(m_ref.shape, NEG_INF, F32)
        l_ref[...] = jnp.zeros(l_ref.shape, F32)
        acc_ref[...] = jnp.zeros(acc_ref.shape, F32)
        carry_ref[...] = jnp.zeros(carry_ref.shape, F32)
        knp_ref[...] = jnp.zeros(knp_ref.shape, F32)
        vnp_ref[...] = jnp.zeros(vnp_ref.shape, F32)
        knp_ref[0:nt, :] = kn_ref[0]
        vnp_ref[0:nt, :] = vn_ref[0]

    qe = qe_ref[...].astype(BF16)
    cn = cn_ref[...]

    def online_update(z, pv):
        m_old = m_ref[...]
        m_new = jnp.maximum(m_old, jnp.max(z, axis=-1, keepdims=True) + cn)
        p = jnp.exp(z - (m_new - cn))
        alpha = jnp.exp(m_old - m_new)
        l_ref[...] = alpha * l_ref[...] + jnp.sum(p, axis=-1, keepdims=True)
        acc_ref[...] = alpha * acc_ref[...] + pv(p.astype(BF16))
        m_ref[...] = m_new

    lf = jnp.concatenate([r[0] for r in lf_refs], axis=1)
    lane_k = lax.broadcasted_iota(jnp.int32, lf.shape, 1)
    suf = lf
    d = 1
    while d < n_keys:
        suf = suf + jnp.where(lane_k < n_keys - d, pltpu.roll(suf, n_keys - d, axis=1), 0.0)
        d *= 2
    carry = carry_ref[:, 0:1]
    rc = (suf - lf) + carry
    carry_ref[...] = jnp.broadcast_to(carry + suf[:, 0:1], carry_ref.shape)
    kt_bf = jnp.concatenate([r[0].astype(BF16) for r in k_refs], axis=1)
    vt_bf = jnp.concatenate([r[0].astype(BF16) for r in v_refs], axis=1)
    z = _dot(qe, kt_bf) + jnp.concatenate([rc] * nt, axis=0)
    online_update(z, lambda p: _dot_nt(p, vt_bf))

    @pl.when(j == pl.num_programs(1) - 1)
    def _():
        z_new = _dot_nt(qe, knp_ref[...].astype(BF16)) - jnp.concatenate([cnl_ref[...]] * nt, axis=0)
        r_id = lax.broadcasted_iota(jnp.int32, (rows, LANES), 0)
        c_id = lax.broadcasted_iota(jnp.int32, (rows, LANES), 1)
        z_new = jnp.where(c_id * HEADS <= r_id, z_new, NEG_INF)
        online_update(z_new, lambda p: _dot(p, vnp_ref[...].astype(BF16)))
        o = acc_ref[...] / l_ref[...]
        for t in range(nt):
            o_t = jnp.where(own_lanes, o[t * HEADS:(t + 1) * HEADS, :], 0.0)
            o_ref[0, t:t + 1, :] = jnp.sum(o_t, axis=0, keepdims=True)


def _attn_decode(page_table, q, k_new, v_new, lf_new_t, cache_k, cache_v, cache_lf_t, *, n_pg):
    nb, nt, w = q.shape
    n_pages = page_table.shape[1]
    n_chunks = n_pages // n_pg
    page = cache_k.shape[2]
    assert page == LANES and cache_k.shape[1] == w
    rows = nt * HEADS

    def seq_spec(shape):
        return pl.BlockSpec((1,) + shape, lambda b, j, pt: (b, 0, 0))

    def page_spec(shape, i):
        return pl.BlockSpec((1,) + shape, lambda b, j, pt: (pt[b, (n_chunks - 1 - j) * n_pg + i], 0, 0))

    in_specs = ([seq_spec((nt, w)) for _ in range(3)] + [seq_spec((HEADS, LANES))]
                + [page_spec((w, page), i) for i in range(n_pg)]
                + [page_spec((w, page), i) for i in range(n_pg)]
                + [page_spec((HEADS, page), i) for i in range(n_pg)])
    grid_spec = pltpu.PrefetchScalarGridSpec(
        num_scalar_prefetch=1, grid=(nb, n_chunks), in_specs=in_specs,
        out_specs=seq_spec((nt, w)),
        scratch_shapes=[pltpu.VMEM((rows, w), F32), pltpu.VMEM((rows, 1), F32),
                        pltpu.VMEM((HEADS, LANES), F32),
                        pltpu.VMEM((rows, 1), F32), pltpu.VMEM((rows, 1), F32),
                        pltpu.VMEM((rows, w), F32), pltpu.VMEM((HEADS, LANES), F32),
                        pltpu.VMEM((LANES, w), F32), pltpu.VMEM((LANES, w), F32)])
    return pl.pallas_call(
        functools.partial(_attn_decode_kernel, n_pg=n_pg, nt=nt),
        grid_spec=grid_spec,
        out_shape=jax.ShapeDtypeStruct((nb, nt, w), F32),
        compiler_params=pltpu.CompilerParams(dimension_semantics=("arbitrary", "arbitrary"),
                                             vmem_limit_bytes=VMEM_LIMIT),
        name="attn_decode",
    )(page_table, q, k_new, v_new, lf_new_t, *([cache_k] * n_pg), *([cache_v] * n_pg),
      *([cache_lf_t] * n_pg))


def _ffn_math(x, ylru_bf, yfox, p, wr, ext_ref, fill_prev, keep_tail, *, pad, unit):
    (gfox_ref, wout_ref, gffn_ref, wup_ref, cw_ref, cb_ref, wdown_ref, wple_ref, wgate_ref,
     gfin_ref) = wr
    n = x.shape[0]
    wl = ylru_bf.shape[1]
    yf = _rms(yfox, gfox_ref[...]).astype(BF16)
    h = x + _dot(ylru_bf, wout_ref[0:wl, :]) + _dot(yf, wout_ref[wl:, :])
    n2 = _rms(h, gffn_ref[...]).astype(BF16)
    d_ff = wdown_ref.shape[0]
    cw = ext_ref.shape[1]
    acc = jnp.zeros(x.shape, F32)
    for c in range(d_ff // cw):
        cs = slice(c * cw, (c + 1) * cw)
        g = _dot(n2, wup_ref[:, cs])
        u = _dot(n2, wup_ref[:, d_ff + c * cw:d_ff + (c + 1) * cw])
        fill_prev(cs)
        ext_ref[pad:pad + n, :] = g
        gc = _causal_conv(ext_ref, cw_ref, cb_ref, cs, pad=pad, unit=unit, n=n)
        keep_tail(cs)
        acc = acc + _dot((_gelu(gc) * u).astype(BF16), wdown_ref[cs, :])
    h = h + acc
    e = _dot(p.astype(BF16), wple_ref[...]) * jax.nn.sigmoid(_dot(h.astype(BF16), wgate_ref[...]))
    return _rms(h + e, gfin_ref[...])


def _ffn_prompt_kernel(x_ref, ylru_ref, yfox_ref, p_ref, *rest, tm):
    wr = rest[:10]
    y_ref, tail_ref, ext_ref, carry_ref = rest[10:]
    t = pl.program_id(1)
    pad = SUBLANES
    taps = wr[4].shape[0]

    @pl.when(t == 0)
    def _():
        carry_ref[...] = jnp.zeros(carry_ref.shape, F32)

    def fill_prev(cs):
        ext_ref[0:pad, :] = carry_ref[:, cs]

    def keep_tail(cs):
        carry_ref[:, cs] = ext_ref[tm:tm + pad, :]
        tail_ref[0, :, cs] = ext_ref[pad + tm - (taps - 1):pad + tm, :]

    y_ref[...] = _ffn_math(x_ref[...], ylru_ref[...], yfox_ref[...], p_ref[...], wr, ext_ref,
                           fill_prev, keep_tail, pad=pad, unit=1)


def _ffn_prompt(x, ylru, yfox, p, weights, *, batch, seq, tm, conv_cols):
    rows, d = x.shape
    nt = seq // tm
    d_ff = weights[6].shape[0]
    taps = weights[4].shape[0]
    tile = lambda c: pl.BlockSpec((tm, c), lambda b, t: (b * nt + t, 0))
    return pl.pallas_call(
        functools.partial(_ffn_prompt_kernel, tm=tm),
        grid=(batch, nt),
        in_specs=[tile(d), tile(ylru.shape[1]), tile(yfox.shape[1]), tile(p.shape[1])]
        + [_const_spec(wt.shape) for wt in weights],
        out_specs=[tile(d), pl.BlockSpec((1, taps - 1, d_ff), lambda b, t: (b, 0, 0))],
        out_shape=[jax.ShapeDtypeStruct((rows, d), F32),
                   jax.ShapeDtypeStruct((batch, taps - 1, d_ff), F32)],
        scratch_shapes=[pltpu.VMEM((SUBLANES + tm, conv_cols), F32),
                        pltpu.VMEM((SUBLANES, d_ff), F32)],
        compiler_params=pltpu.CompilerParams(dimension_semantics=("arbitrary", "arbitrary"),
                                             vmem_limit_bytes=VMEM_LIMIT),
        name="ffn_prompt",
    )(x, ylru, yfox, p, *weights)


def _ffn_sample_kernel(x_ref, ylru_ref, yfox_ref, p_ref, st_ref, *rest, nt):
    wr = rest[:10]
    y_ref, tail_ref, ext_ref = rest[10:]
    nb = x_ref.shape[0]
    d = x_ref.shape[1] // nt
    d_ff = wr[6].shape[0]
    taps = wr[4].shape[0]
    pad = (taps - 1) * nb

    def fill_prev(cs):
        for j in range(taps - 1):
            ext_ref[j * nb:(j + 1) * nb, :] = st_ref[:, j * d_ff + cs.start:j * d_ff + cs.stop]

    def keep_tail(cs):
        for j in range(taps - 1):
            tail_ref[:, j * d_ff + cs.start:j * d_ff + cs.stop] = ext_ref[(nt + j) * nb:(nt + j + 1) * nb, :]

    y = _ffn_math(_time_major(x_ref, nt), _time_major(ylru_ref, nt), _time_major(yfox_ref, nt),
                  _time_major(p_ref, nt), wr, ext_ref, fill_prev, keep_tail, pad=pad, unit=nb)
    for t in range(nt):
        y_ref[:, t * d:(t + 1) * d] = y[t * nb:(t + 1) * nb, :]


def _ffn_sample(x, ylru, yfox, p, state, weights, *, nt, conv_cols):
    nb = x.shape[0]
    taps = weights[4].shape[0]
    args = [x, ylru, yfox, p, state] + list(weights)
    return pl.pallas_call(
        functools.partial(_ffn_sample_kernel, nt=nt),
        grid=(1,),
        in_specs=[_const_spec(a.shape) for a in args],
        out_specs=[_whole_spec(x.shape), _whole_spec(state.shape)],
        out_shape=[jax.ShapeDtypeStruct(x.shape, F32), jax.ShapeDtypeStruct(state.shape, F32)],
        scratch_shapes=[pltpu.VMEM(((taps - 1 + nt) * nb, conv_cols), F32)],
        compiler_params=pltpu.CompilerParams(dimension_semantics=("arbitrary",),
                                             vmem_limit_bytes=VMEM_LIMIT),
        name="ffn_sample",
    )(*args)


def _block_diag(w):
    nblk, k, j = w.shape
    eye = jnp.eye(nblk, dtype=w.dtype)
    return (w[:, :, None, :] * eye[:, None, :, None]).reshape(nblk * k, nblk * j)


def kernel(x_prompt, x_sample, p_prompt, p_sample, cache_k, cache_v, cache_logf, state_lru_h, state_lru_conv, state_ffn_conv, page_table, g_mix, w_in, lru_conv_w, lru_conv_b, lru_w_a, lru_b_a, lru_w_x, lru_b_x, lru_lambda, fox_b_f, g_lru_out, g_fox_out, w_out, g_ffn, w_up, ffn_conv_w, ffn_conv_b, w_down, w_ple, w_ple_gate, g_final):
    depth = w_in.shape[0]
    assert depth == 1, "single-layer step"
    bp, sp, d = x_prompt.shape
    bs, ts, _ = x_sample.shape
    lw = state_lru_h.shape[-1]
    fw = HEADS * HEAD_DIM
    d_ff = w_down.shape[1]
    n_phys, page = cache_k.shape[1], cache_k.shape[2]
    l = 0
    row = lambda a: a.reshape(1, -1)

    w_main = w_in[l][:, :2 * lw + 3 * fw].astype(BF16)
    w_f = jnp.pad(w_in[l][:, 2 * lw + 3 * fw:], ((0, 0), (0, LANES - HEADS))).astype(BF16)
    b_f = jnp.pad(fox_b_f[l], (0, LANES - HEADS)).reshape(1, LANES)
    lru_w = [lru_conv_w[l], row(lru_conv_b[l]), _block_diag(lru_w_a[l]).astype(BF16), row(lru_b_a[l]),
             _block_diag(lru_w_x[l]).astype(BF16), row(lru_b_x[l]), row(lru_lambda[l]), row(g_lru_out[l])]
    ffn_w = [row(g_fox_out[l]), w_out[l].astype(BF16), row(g_ffn[l]), w_up[l].astype(BF16),
             ffn_conv_w[l], row(ffn_conv_b[l]), w_down[l].astype(BF16), w_ple[l].astype(BF16),
             w_ple_gate[l].astype(BF16), row(g_final)]
    g_mix_r = row(g_mix[l])
    w_kvt = jnp.swapaxes(w_in[l][:, 2 * lw + fw:2 * lw + 3 * fw], 0, 1).astype(BF16)

    xp = x_prompt.reshape(bp * sp, d)
    xr, gr, q_bf, kt_p, vt_p, kt_bf, vt_bf, lf_pad = _inproj_prompt(
        xp, g_mix_r, w_main, w_kvt, w_f, b_f, batch=bp, seq=sp, tm=512)
    lft_p, c_rows, c_t = _cumsum(lf_pad, bp, sp)
    ylru_p, hlast_p = _lru_prompt(xr, gr, *lru_w, batch=bp, seq=sp, tt=512)
    yfox_p = _attn_prompt(q_bf, kt_bf, vt_bf, c_rows, c_t.reshape(bp * HEADS, sp),
                          batch=bp, seq=sp, tq=1024, tk=512)
    y_p, ffn_tail_p = _ffn_prompt(xp, ylru_p, yfox_p, p_prompt[l].reshape(bp * sp, -1), ffn_w,
                                  batch=bp, seq=sp, tm=512, conv_cols=512)
    to_bshd = lambda a: jnp.transpose(a.reshape(1, bp, HEADS, HEAD_DIM, sp), (0, 1, 4, 2, 3))
    k_p, v_p = to_bshd(kt_p), to_bshd(vt_p)
    logf_p = jnp.transpose(lft_p, (0, 2, 1))[None]

    xs = x_sample.reshape(bs, ts * d)
    xr_s, gr_s, q_s, k_s, v_s, lf_s = _inproj_sample(xs, g_mix_r, w_main, w_f, b_f, nt=ts)
    logf_s = lf_s.reshape(bs, ts, LANES)[:, :, :HEADS]
    ylru_s, hlast_s = _lru_sample(
        xr_s, gr_s, state_lru_h[l], state_lru_conv[l].reshape(bs, -1), *lru_w, nt=ts)
    pages_t = lambda c: jnp.transpose(c, (0, 2, 3, 1)).reshape(n_phys, fw, page)
    yfox_s = _attn_decode(
        page_table, q_s.reshape(bs, ts, fw), k_s.reshape(bs, ts, fw), v_s.reshape(bs, ts, fw),
        jnp.pad(jnp.swapaxes(logf_s, 1, 2), ((0, 0), (0, 0), (0, LANES - ts))),
        pages_t(cache_k[l]), pages_t(cache_v[l]), jnp.swapaxes(cache_logf[l], 1, 2), n_pg=16)
    y_s, ffn_tail_s = _ffn_sample(
        xs, ylru_s, yfox_s.reshape(bs, ts * fw), p_sample[l].reshape(bs, -1),
        state_ffn_conv[l].reshape(bs, -1), ffn_w, nt=ts, conv_cols=512)

    lru_taps = lru_conv_w.shape[1]
    return (y_p.reshape(bp, sp, d), y_s.reshape(bs, ts, d),
            k_p, v_p, logf_p, hlast_p.reshape(1, bp, lw),
            xr.reshape(bp, sp, lw)[None, :, sp - (lru_taps - 1):, :], ffn_tail_p[None],
            k_s.reshape(1, bs, ts, HEADS, HEAD_DIM), v_s.reshape(1, bs, ts, HEADS, HEAD_DIM),
            logf_s[None], hlast_s[None],
            xr_s.reshape(bs, ts, lw)[None, :, ts - (lru_taps - 1):, :],
            ffn_tail_s.reshape(1, bs, -1, d_ff))
```

```python
import functools
import math

import jax
import jax.numpy as jnp
from jax import lax
from jax.experimental import pallas as pl
from jax.experimental.pallas import tpu as pltpu

F32 = jnp.float32
BF16 = jnp.bfloat16

EPS = 1e-6
LRU_C = 8.0
HEAD_DIM = 64
HEADS = 8
LANES = 128
SUBLANES = 8
VMEM_LIMIT = 56 * 1024 * 1024
NEG_INF = float("-inf")
GELU_C = math.sqrt(2.0 / math.pi)
LOG2_E = math.log2(math.e)


def _rms(x, g):
    ms = jnp.mean(x * x, axis=-1, keepdims=True)
    return x * lax.rsqrt(ms + EPS) * g


def _gelu(x):
    return x * (0.5 * (1.0 + jnp.tanh(GELU_C * (x + 0.044715 * (x * x * x)))))


def _softplus(x):
    return jnp.maximum(x, 0.0) + jnp.log1p(jnp.exp(-jnp.abs(x)))


def _dot(a, b):
    return jnp.dot(a, b, preferred_element_type=F32)


def _dot_nt(a, b):
    return lax.dot_general(a, b, (((1,), (1,)), ((), ())), preferred_element_type=F32)


def _const_spec(shape):
    nd = len(shape)
    return pl.BlockSpec(shape, lambda *_: (0,) * nd, pipeline_mode=pl.Buffered(1))


def _whole_spec(shape):
    nd = len(shape)
    return pl.BlockSpec(shape, lambda *_: (0,) * nd)


def _time_major(ref, nt):
    c = ref.shape[1] // nt
    return jnp.concatenate([ref[:, t * c:(t + 1) * c] for t in range(nt)], axis=0)


def _inproj_sample_kernel(x_ref, g_ref, wm_ref, wf_ref, bf_ref,
                          xr_ref, gr_ref, q_ref, k_ref, v_ref, lf_ref, *, nt):
    d = x_ref.shape[1] // nt
    w = xr_ref.shape[1] // nt
    for t in range(nt):
        n = _rms(x_ref[:, t * d:(t + 1) * d], g_ref[...]).astype(BF16)
        ws = slice(t * w, (t + 1) * w)
        xr_ref[:, ws] = _dot(n, wm_ref[:, 0 * w:1 * w])
        gr_ref[:, ws] = _dot(n, wm_ref[:, 1 * w:2 * w])
        q_ref[:, ws] = _dot(n, wm_ref[:, 2 * w:3 * w]) * (HEAD_DIM ** -0.5)
        k_ref[:, ws] = _dot(n, wm_ref[:, 3 * w:4 * w])
        v_ref[:, ws] = _dot(n, wm_ref[:, 4 * w:5 * w])
        fl = _dot(n, wf_ref[...]) + bf_ref[...]
        lf_ref[:, t * LANES:(t + 1) * LANES] = -_softplus(-fl)


def _inproj_sample(x, g_mix, w_main, w_f, b_f, *, nt):
    rows = x.shape[0]
    w = w_main.shape[1] // 5
    args = [x, g_mix, w_main, w_f, b_f]
    shapes = [(rows, nt * w)] * 5 + [(rows, nt * LANES)]
    return pl.pallas_call(
        functools.partial(_inproj_sample_kernel, nt=nt),
        grid=(1,),
        in_specs=[_const_spec(a.shape) for a in args],
        out_specs=[_whole_spec(s) for s in shapes],
        out_shape=[jax.ShapeDtypeStruct(s, F32) for s in shapes],
        compiler_params=pltpu.CompilerParams(dimension_semantics=("arbitrary",),
                                             vmem_limit_bytes=VMEM_LIMIT),
        name="inproj_sample",
    )(*args)


def _inproj_prompt_kernel(x_ref, g_ref, wm_ref, wkvt_ref, wf_ref, bf_ref,
                          xr_ref, gr_ref, q_ref, kt_ref, vt_ref, ktb_ref, vtb_ref, lf_ref):
    n = _rms(x_ref[...], g_ref[...]).astype(BF16)
    w = xr_ref.shape[1]
    xr_ref[...] = _dot(n, wm_ref[:, 0 * w:1 * w])
    gr_ref[...] = _dot(n, wm_ref[:, 1 * w:2 * w])
    q_ref[...] = (_dot(n, wm_ref[:, 2 * w:3 * w]) * (HEAD_DIM ** -0.5 * LOG2_E)).astype(q_ref.dtype)
    kt = _dot_nt(wkvt_ref[0:w, :], n)
    vt = _dot_nt(wkvt_ref[w:2 * w, :], n)
    kt_ref[...] = kt
    vt_ref[...] = vt
    ktb_ref[...] = kt.astype(BF16)
    vtb_ref[...] = vt.astype(BF16)
    fl = _dot(n, wf_ref[...]) + bf_ref[...]
    lf_ref[...] = -_softplus(-fl)


def _inproj_prompt(x, g_mix, w_main, w_kvt, w_f, b_f, *, batch, seq, tm):
    rows, d = x.shape
    w = w_kvt.shape[0] // 2
    nt = seq // tm
    row_spec = lambda c: pl.BlockSpec((tm, c), lambda b, t: (b * nt + t, 0))
    col_spec = pl.BlockSpec((w, tm), lambda b, t: (b, t))
    consts = [g_mix, w_main, w_kvt, w_f, b_f]
    return pl.pallas_call(
        _inproj_prompt_kernel,
        grid=(batch, nt),
        in_specs=[row_spec(d)] + [_const_spec(c.shape) for c in consts],
        out_specs=[row_spec(w), row_spec(w), row_spec(w), col_spec, col_spec, col_spec, col_spec,
                   row_spec(LANES)],
        out_shape=[jax.ShapeDtypeStruct((rows, w), F32), jax.ShapeDtypeStruct((rows, w), F32),
                   jax.ShapeDtypeStruct((rows, w), BF16),
                   jax.ShapeDtypeStruct((batch * w, seq), F32), jax.ShapeDtypeStruct((batch * w, seq), F32),
                   jax.ShapeDtypeStruct((batch * w, seq), BF16), jax.ShapeDtypeStruct((batch * w, seq), BF16),
                   jax.ShapeDtypeStruct((rows, LANES), F32)],
        compiler_params=pltpu.CompilerParams(dimension_semantics=("arbitrary", "arbitrary"),
                                             vmem_limit_bytes=VMEM_LIMIT),
        name="inproj_prompt",
    )(x, *consts)


def _cumsum_kernel(lf_ref, lft_ref, c_ref, ct_ref):
    x = lf_ref[...]
    lft_ref[0] = x.T[:HEADS, :]
    n = x.shape[0]
    row = lax.broadcasted_iota(jnp.int32, x.shape, 0)
    d = 1
    while d < n:
        x = x + jnp.where(row >= d, pltpu.roll(x, d, axis=0), 0.0)
        d *= 2
    x = x * LOG2_E
    c_ref[...] = x
    ct_ref[0] = x.T[:HEADS, :]


def _cumsum(lf_pad, batch, seq):
    rows = lf_pad.shape[0]
    head_major = pl.BlockSpec((1, HEADS, seq), lambda b: (b, 0, 0))
    return pl.pallas_call(
        _cumsum_kernel,
        grid=(batch,),
        in_specs=[pl.BlockSpec((seq, LANES), lambda b: (b, 0))],
        out_specs=[head_major, pl.BlockSpec((seq, LANES), lambda b: (b, 0)), head_major],
        out_shape=[jax.ShapeDtypeStruct((batch, HEADS, seq), F32),
                   jax.ShapeDtypeStruct((rows, LANES), F32),
                   jax.ShapeDtypeStruct((batch, HEADS, seq), F32)],
        compiler_params=pltpu.CompilerParams(dimension_semantics=("arbitrary",),
                                             vmem_limit_bytes=VMEM_LIMIT),
        name="logf_cumsum",
    )(lf_pad)


def _causal_conv(ext_ref, cw_ref, cb_ref, cols, *, pad, unit, n):
    taps = cw_ref.shape[0]
    out = cb_ref[:, cols]
    for j in range(taps):
        out = out + cw_ref[taps - 1 - j:taps - j, cols] * ext_ref[pl.ds(pad - j * unit, n), :]
    return out


def _lru_gates(xc, wa_ref, ba_ref, wx_ref, bx_ref, lam_ref):
    xcb = xc.astype(BF16)
    r = jax.nn.sigmoid(_dot(xcb, wa_ref[...]) + ba_ref[...])
    i = jax.nn.sigmoid(_dot(xcb, wx_ref[...]) + bx_ref[...])
    log_a = (-LRU_C) * r * _softplus(-lam_ref[...])
    a = jnp.exp(log_a)
    mult = jnp.sqrt(-jnp.tanh(log_a) * (a * a + 1.0))
    return a, mult, i * xc


def _lru_prompt_kernel(xr_ref, gr_ref, cw_ref, cb_ref, wa_ref, ba_ref, wx_ref, bx_ref, lam_ref,
                       g_ref, y_ref, hlast_ref, ext_ref, a_ref, u_ref, hs_ref, h_ref, *, tt):
    t = pl.program_id(1)
    pad = SUBLANES

    @pl.when(t == 0)
    def _():
        ext_ref[0:pad, :] = jnp.zeros((pad, ext_ref.shape[1]), F32)
        h_ref[...] = jnp.zeros(h_ref.shape, F32)

    x = xr_ref[...]
    ext_ref[pad:pad + tt, :] = x
    xc = _causal_conv(ext_ref, cw_ref, cb_ref, slice(None), pad=pad, unit=1, n=tt)
    ext_ref[0:pad, :] = x[tt - pad:tt, :]

    a, mult, ix = _lru_gates(xc, wa_ref, ba_ref, wx_ref, bx_ref, lam_ref)
    row = lax.broadcasted_iota(jnp.int32, (tt, 1), 0)
    mult = jnp.where((row == 0) & (t == 0), 1.0, mult)
    a_ref[...] = a
    u_ref[...] = mult * ix

    row8 = lax.broadcasted_iota(jnp.int32, (SUBLANES, a_ref.shape[1]), 0)

    def group(gi, h):
        r0 = pl.multiple_of(gi * SUBLANES, SUBLANES)
        ag = a_ref[pl.ds(r0, SUBLANES), :]
        ug = u_ref[pl.ds(r0, SUBLANES), :]
        for d in (1, 2, 4):
            a_sh = jnp.where(row8 >= d, pltpu.roll(ag, d, axis=0), 1.0)
            u_sh = jnp.where(row8 >= d, pltpu.roll(ug, d, axis=0), 0.0)
            ug = ag * u_sh + ug
            ag = ag * a_sh
        hs = ag * h + ug
        hs_ref[pl.ds(r0, SUBLANES), :] = hs
        return jnp.broadcast_to(hs[SUBLANES - 1:SUBLANES, :], hs.shape)

    h = lax.fori_loop(0, tt // SUBLANES, group, h_ref[...], unroll=2)
    h_ref[...] = h
    hlast_ref[0] = h[0:1, :]
    y = hs_ref[...] * _gelu(gr_ref[...])
    y_ref[...] = _rms(y, g_ref[...]).astype(y_ref.dtype)


def _lru_prompt(xr, gr, cw, cb, wa, ba, wx, bx, lam, g_out, *, batch, seq, tt):
    rows, w = xr.shape
    nt = seq // tt
    tile = pl.BlockSpec((tt, w), lambda b, t: (b * nt + t, 0))
    consts = [cw, cb, wa, ba, wx, bx, lam, g_out]
    return pl.pallas_call(
        functools.partial(_lru_prompt_kernel, tt=tt),
        grid=(batch, nt),
        in_specs=[tile, tile] + [_const_spec(c.shape) for c in consts],
        out_specs=[tile, pl.BlockSpec((1, 1, w), lambda b, t: (b, 0, 0))],
        out_shape=[jax.ShapeDtypeStruct((rows, w), BF16), jax.ShapeDtypeStruct((batch, 1, w), F32)],
        scratch_shapes=[pltpu.VMEM((SUBLANES + tt, w), F32), pltpu.VMEM((tt, w), F32),
                        pltpu.VMEM((tt, w), F32), pltpu.VMEM((tt, w), F32),
                        pltpu.VMEM((SUBLANES, w), F32)],
        compiler_params=pltpu.CompilerParams(dimension_semantics=("arbitrary", "arbitrary"),
                                             vmem_limit_bytes=VMEM_LIMIT),
        name="lru_prompt",
    )(xr, gr, *consts)


def _lru_sample_kernel(xr_ref, gr_ref, h0_ref, cbuf_ref, cw_ref, cb_ref, wa_ref, ba_ref, wx_ref,
                       bx_ref, lam_ref, g_ref, y_ref, hlast_ref, ext_ref, *, nt):
    nb, w = h0_ref.shape
    taps = cw_ref.shape[0]
    pad = (taps - 1) * nb
    n = nb * nt
    ext_ref[0:pad, :] = _time_major(cbuf_ref, taps - 1)
    ext_ref[pad:pad + n, :] = _time_major(xr_ref, nt)
    xc = _causal_conv(ext_ref, cw_ref, cb_ref, slice(None), pad=pad, unit=nb, n=n)
    a, mult, ix = _lru_gates(xc, wa_ref, ba_ref, wx_ref, bx_ref, lam_ref)
    u = mult * ix
    h = h0_ref[...]
    for t in range(nt):
        h = a[t * nb:(t + 1) * nb, :] * h + u[t * nb:(t + 1) * nb, :]
        y = h * _gelu(gr_ref[:, t * w:(t + 1) * w])
        y_ref[:, t * w:(t + 1) * w] = _rms(y, g_ref[...]).astype(y_ref.dtype)
    hlast_ref[...] = h


def _lru_sample(xr, gr, h0, cbuf, cw, cb, wa, ba, wx, bx, lam, g_out, *, nt):
    nb, w = h0.shape
    args = [xr, gr, h0, cbuf, cw, cb, wa, ba, wx, bx, lam, g_out]
    taps = cw.shape[0]
    return pl.pallas_call(
        functools.partial(_lru_sample_kernel, nt=nt),
        grid=(1,),
        in_specs=[_const_spec(a.shape) for a in args],
        out_specs=[_whole_spec((nb, nt * w)), _whole_spec((nb, w))],
        out_shape=[jax.ShapeDtypeStruct((nb, nt * w), BF16), jax.ShapeDtypeStruct((nb, w), F32)],
        scratch_shapes=[pltpu.VMEM(((taps - 1 + nt) * nb, w), F32)],
        compiler_params=pltpu.CompilerParams(dimension_semantics=("arbitrary",),
                                             vmem_limit_bytes=VMEM_LIMIT),
        name="lru_sample",
    )(*args)


def _attn_prompt_kernel(q_ref, kt_ref, vt_ref, c_ref, ct_ref, o_ref, ck_ref, *, tq, tk):
    hp = pl.program_id(1)
    qi = pl.program_id(2)
    heads_per_blk = LANES // HEAD_DIM
    q = q_ref[...]
    c_blk = c_ref[...]
    lane = lax.broadcasted_iota(jnp.int32, (tq, LANES), 1)
    n_full = (qi * tq) // tk
    n_diag = pl.cdiv(tq, tk)
    outs = []
    for hh in range(heads_per_blk):
        h = hp * heads_per_blk + hh
        in_head = (lane >= hh * HEAD_DIM) & (lane < (hh + 1) * HEAD_DIM)
        qm = jnp.where(in_head, q, jnp.zeros_like(q))
        cq = jnp.sum(jnp.where(lane == h, c_blk, 0.0), axis=-1, keepdims=True)
        ck_ref[...] = ct_ref[pl.ds(h, 1), :]

        def tile(j, carry, masked):
            m, l, acc = carry
            ks = pl.multiple_of(j * tk, tk)
            z = _dot(qm, kt_ref[:, pl.ds(ks, tk)]) - ck_ref[:, pl.ds(ks, tk)]
            if masked:
                row_i = lax.broadcasted_iota(jnp.int32, (tq, tk), 0) + qi * tq
                col_i = lax.broadcasted_iota(jnp.int32, (tq, tk), 1) + j * tk
                z = jnp.where(col_i <= row_i, z, NEG_INF)
            m_new = jnp.maximum(m, jnp.max(z, axis=-1, keepdims=True) + cq)
            p = jnp.exp2(z - (m_new - cq))
            alpha = jnp.exp2(m - m_new)
            l = alpha * l + jnp.sum(p, axis=-1, keepdims=True)
            acc = alpha * acc + _dot_nt(p.astype(BF16), vt_ref[:, pl.ds(ks, tk)])
            return m_new, l, acc

        carry = (jnp.full((tq, 1), NEG_INF, F32), jnp.zeros((tq, 1), F32), jnp.zeros((tq, LANES), F32))
        carry = lax.fori_loop(0, n_full, functools.partial(tile, masked=False), carry)
        for dj in range(n_diag):
            carry = tile(n_full + dj, carry, True)
        m, l, acc = carry
        outs.append((in_head, acc / l))
    o_ref[...] = jnp.where(outs[0][0], outs[0][1], outs[1][1])


def _attn_prompt(q_bf, kt_bf, vt_bf, c_rows, c_t, *, batch, seq, tq, tk):
    rows, w = q_bf.shape
    nq = seq // tq
    n_hp = w // LANES
    kv_spec = pl.BlockSpec((LANES, seq), lambda b, hp, qi: (b * n_hp + hp, 0))
    return pl.pallas_call(
        functools.partial(_attn_prompt_kernel, tq=tq, tk=tk),
        grid=(batch, n_hp, nq),
        in_specs=[pl.BlockSpec((tq, LANES), lambda b, hp, qi: (b * nq + qi, hp)),
                  kv_spec, kv_spec,
                  pl.BlockSpec((tq, LANES), lambda b, hp, qi: (b * nq + qi, 0)),
                  pl.BlockSpec((HEADS, seq), lambda b, hp, qi: (b, 0))],
        out_specs=pl.BlockSpec((tq, LANES), lambda b, hp, qi: (b * nq + qi, hp)),
        out_shape=jax.ShapeDtypeStruct((rows, w), F32),
        scratch_shapes=[pltpu.VMEM((1, seq), F32)],
        compiler_params=pltpu.CompilerParams(
            dimension_semantics=("arbitrary", "arbitrary", "arbitrary"),
            vmem_limit_bytes=VMEM_LIMIT),
        name="attn_prompt",
    )(q_bf, kt_bf, vt_bf, c_rows, c_t)


def _attn_decode_kernel(pt_ref, q_ref, kn_ref, vn_ref, lfn_ref, ck_hbm, cv_hbm, clf_hbm, o_ref,
                        kbuf, vbuf, lfbuf, sem, qe_ref, cn_ref, cnl_ref, m_ref, l_ref, acc_ref,
                        carry_ref, knp_ref, vnp_ref, *, n_pg, nt):
    b = pl.program_id(0)
    j = pl.program_id(1)
    n_chunks = pl.num_programs(1)
    step = b * n_chunks + j
    n_steps = pl.num_programs(0) * n_chunks
    slot = step % 2

    def chunk_copies(s, buf_slot):
        first_page = (n_chunks - 1 - s % n_chunks) * n_pg
        copies = []
        for i in range(n_pg):
            pg = pt_ref[s // n_chunks, first_page + i]
            copies.append(pltpu.make_async_copy(ck_hbm.at[pg], kbuf.at[buf_slot, i], sem.at[0, buf_slot]))
            copies.append(pltpu.make_async_copy(cv_hbm.at[pg], vbuf.at[buf_slot, i], sem.at[1, buf_slot]))
            copies.append(pltpu.make_async_copy(clf_hbm.at[pg], lfbuf.at[buf_slot, i], sem.at[2, buf_slot]))
        return copies

    @pl.when(step == 0)
    def _():
        for c in chunk_copies(0, 0):
            c.start()

    @pl.when(step + 1 < n_steps)
    def _():
        for c in chunk_copies(step + 1, 1 - slot):
            c.start()

    for c in chunk_copies(step, slot):
        c.wait()
    k_refs = [kbuf.at[slot, i] for i in range(n_pg)]
    v_refs = [vbuf.at[slot, i] for i in range(n_pg)]
    lf_refs = [lfbuf.at[slot, i] for i in range(n_pg)]
    w = q_ref.shape[2]
    n_keys = n_pg * LANES
    rows = nt * HEADS
    lane_p = lax.broadcasted_iota(jnp.int32, (HEADS, LANES), 1)
    sub = lax.broadcasted_iota(jnp.int32, (HEADS, w), 0)
    lane_w = lax.broadcasted_iota(jnp.int32, (HEADS, w), 1)
    own_lanes = (lane_w >= sub * HEAD_DIM) & (lane_w < (sub + 1) * HEAD_DIM)

    @pl.when(j == 0)
    def _():
        q = q_ref[0]
        lfn = lfn_ref[0]
        cn = jnp.zeros((HEADS, 1), F32)
        cn_lanes = jnp.zeros((HEADS, LANES), F32)
        for t in range(nt):
            qe_ref[t * HEADS:(t + 1) * HEADS, :] = jnp.where(
                own_lanes, jnp.broadcast_to(q[t:t + 1, :], (HEADS, w)), 0.0)
            cn = cn + jnp.sum(jnp.where(lane_p == t, lfn, 0.0), axis=1, keepdims=True)
            cn_ref[t * HEADS:(t + 1) * HEADS, :] = cn
            cn_lanes = jnp.where(lane_p == t, cn, cn_lanes)
        cnl_ref[...] = cn_lanes
        m_ref[...] = jnp.full(m_ref.shape, NEG_INF, F32)
        l_ref[...] = jnp.zeros(l_ref.shape, F32)
        acc_ref[...] = jnp.zeros(acc_ref.shape, F32)
        carry_ref[...] = jnp.zeros(carry_ref.shape, F32)
        knp_ref[...] = jnp.zeros(knp_ref.shape, F32)
        vnp_ref[...] = jnp.zeros(vnp_ref.shape, F32)
        knp_ref[0:nt, :] = kn_ref[0]
        vnp_ref[0:nt, :] = vn_ref[0]

    qe = qe_ref[...].astype(BF16)
    cn = cn_ref[...]

    def online_update(z, pv):
        m_old = m_ref[...]
        m_new = jnp.maximum(m_old, jnp.max(z, axis=-1, keepdims=True) + cn)
        p = jnp.exp(z - (m_new - cn))
        alpha = jnp.exp(m_old - m_new)
        l_ref[...] = alpha * l_ref[...] + jnp.sum(p, axis=-1, keepdims=True)
        acc_ref[...] = alpha * acc_ref[...] + pv(p.astype(BF16))
        m_ref[...] = m_new

    lf = jnp.concatenate([r[...] for r in lf_refs], axis=1)
    lane_k = lax.broadcasted_iota(jnp.int32, lf.shape, 1)
    suf = lf
    d = 1
    while d < n_keys:
        suf = suf + jnp.where(lane_k < n_keys - d, pltpu.roll(suf, n_keys - d, axis=1), 0.0)
        d *= 2
    carry = carry_ref[:, 0:1]
    rc = (suf - lf) + carry
    carry_ref[...] = jnp.broadcast_to(carry + suf[:, 0:1], carry_ref.shape)
    kt_bf = jnp.concatenate([r[...].astype(BF16) for r in k_refs], axis=1)
    vt_bf = jnp.concatenate([r[...].astype(BF16) for r in v_refs], axis=1)
    z = _dot(qe, kt_bf) + jnp.concatenate([rc] * nt, axis=0)
    online_update(z, lambda p: _dot_nt(p, vt_bf))

    @pl.when(j == pl.num_programs(1) - 1)
    def _():
        z_new = _dot_nt(qe, knp_ref[...].astype(BF16)) - jnp.concatenate([cnl_ref[...]] * nt, axis=0)
        r_id = lax.broadcasted_iota(jnp.int32, (rows, LANES), 0)
        c_id = lax.broadcasted_iota(jnp.int32, (rows, LANES), 1)
        z_new = jnp.where(c_id * HEADS <= r_id, z_new, NEG_INF)
        online_update(z_new, lambda p: _dot(p, vnp_ref[...].astype(BF16)))
        o = acc_ref[...] / l_ref[...]
        for t in range(nt):
            o_t = jnp.where(own_lanes, o[t * HEADS:(t + 1) * HEADS, :], 0.0)
            o_ref[0, t:t + 1, :] = jnp.sum(o_t, axis=0, keepdims=True)


def _attn_decode(page_table, q, k_new, v_new, lf_new_t, cache_k, cache_v, cache_lf_t, *, n_pg):
    nb, nt, w = q.shape
    n_pages = page_table.shape[1]
    n_chunks = n_pages // n_pg
    page = cache_k.shape[2]
    assert page == LANES and cache_k.shape[1] == w
    rows = nt * HEADS

    def seq_spec(shape):
        return pl.BlockSpec((1,) + shape, lambda b, j, pt: (b, 0, 0))

    in_specs = ([seq_spec((nt, w)) for _ in range(3)] + [seq_spec((HEADS, LANES))]
                + [pl.BlockSpec(memory_space=pl.ANY) for _ in range(3)])
    grid_spec = pltpu.PrefetchScalarGridSpec(
        num_scalar_prefetch=1, grid=(nb, n_chunks), in_specs=in_specs,
        out_specs=seq_spec((nt, w)),
        scratch_shapes=[pltpu.VMEM((2, n_pg, w, page), F32), pltpu.VMEM((2, n_pg, w, page), F32),
                        pltpu.VMEM((2, n_pg, HEADS, page), F32), pltpu.SemaphoreType.DMA((3, 2)),
                        pltpu.VMEM((rows, w), F32), pltpu.VMEM((rows, 1), F32),
                        pltpu.VMEM((HEADS, LANES), F32),
                        pltpu.VMEM((rows, 1), F32), pltpu.VMEM((rows, 1), F32),
                        pltpu.VMEM((rows, w), F32), pltpu.VMEM((HEADS, LANES), F32),
                        pltpu.VMEM((LANES, w), F32), pltpu.VMEM((LANES, w), F32)])
    return pl.pallas_call(
        functools.partial(_attn_decode_kernel, n_pg=n_pg, nt=nt),
        grid_spec=grid_spec,
        out_shape=jax.ShapeDtypeStruct((nb, nt, w), F32),
        compiler_params=pltpu.CompilerParams(dimension_semantics=("arbitrary", "arbitrary"),
                                             vmem_limit_bytes=VMEM_LIMIT),
        name="attn_decode",
    )(page_table, q, k_new, v_new, lf_new_t, cache_k, cache_v, cache_lf_t)


def _ffn_math(x, ylru_bf, yfox, p, wr, ext_ref, fill_prev, keep_tail, *, pad, unit):
    (gfox_ref, wout_ref, gffn_ref, wup_ref, cw_ref, cb_ref, wdown_ref, wple_ref, wgate_ref,
     gfin_ref) = wr
    n = x.shape[0]
    wl = ylru_bf.shape[1]
    yf = _rms(yfox, gfox_ref[...]).astype(BF16)
    h = x + _dot(ylru_bf, wout_ref[0:wl, :]) + _dot(yf, wout_ref[wl:, :])
    n2 = _rms(h, gffn_ref[...]).astype(BF16)
    d_ff = wdown_ref.shape[0]
    cw = ext_ref.shape[1]
    acc = jnp.zeros(x.shape, F32)
    for c in range(d_ff // cw):
        cs = slice(c * cw, (c + 1) * cw)
        g = _dot(n2, wup_ref[:, cs])
        u = _dot(n2, wup_ref[:, d_ff + c * cw:d_ff + (c + 1) * cw])
        fill_prev(cs)
        ext_ref[pad:pad + n, :] = g
        gc = _causal_conv(ext_ref, cw_ref, cb_ref, cs, pad=pad, unit=unit, n=n)
        keep_tail(cs)
        acc = acc + _dot((_gelu(gc) * u).astype(BF16), wdown_ref[cs, :])
    h = h + acc
    e = _dot(p.astype(BF16), wple_ref[...]) * jax.nn.sigmoid(_dot(h.astype(BF16), wgate_ref[...]))
    return _rms(h + e, gfin_ref[...])


def _ffn_prompt_kernel(x_ref, ylru_ref, yfox_ref, p_ref, *rest, tm):
    wr = rest[:10]
    y_ref, tail_ref, ext_ref, carry_ref = rest[10:]
    t = pl.program_id(1)
    pad = SUBLANES
    taps = wr[4].shape[0]

    @pl.when(t == 0)
    def _():
        carry_ref[...] = jnp.zeros(carry_ref.shape, F32)

    def fill_prev(cs):
        ext_ref[0:pad, :] = carry_ref[:, cs]

    def keep_tail(cs):
        carry_ref[:, cs] = ext_ref[tm:tm + pad, :]
        tail_ref[0, :, cs] = ext_ref[pad + tm - (taps - 1):pad + tm, :]

    y_ref[...] = _ffn_math(x_ref[...], ylru_ref[...], yfox_ref[...], p_ref[...], wr, ext_ref,
                           fill_prev, keep_tail, pad=pad, unit=1)


def _ffn_prompt(x, ylru, yfox, p, weights, *, batch, seq, tm, conv_cols):
    rows, d = x.shape
    nt = seq // tm
    d_ff = weights[6].shape[0]
    taps = weights[4].shape[0]
    tile = lambda c: pl.BlockSpec((tm, c), lambda b, t: (b * nt + t, 0))
    return pl.pallas_call(
        functools.partial(_ffn_prompt_kernel, tm=tm),
        grid=(batch, nt),
        in_specs=[tile(d), tile(ylru.shape[1]), tile(yfox.shape[1]), tile(p.shape[1])]
        + [_const_spec(wt.shape) for wt in weights],
        out_specs=[tile(d), pl.BlockSpec((1, taps - 1, d_ff), lambda b, t: (b, 0, 0))],
        out_shape=[jax.ShapeDtypeStruct((rows, d), F32),
                   jax.ShapeDtypeStruct((batch, taps - 1, d_ff), F32)],
        scratch_shapes=[pltpu.VMEM((SUBLANES + tm, conv_cols), F32),
                        pltpu.VMEM((SUBLANES, d_ff), F32)],
        compiler_params=pltpu.CompilerParams(dimension_semantics=("arbitrary", "arbitrary"),
                                             vmem_limit_bytes=VMEM_LIMIT),
        name="ffn_prompt",
    )(x, ylru, yfox, p, *weights)


def _ffn_sample_kernel(x_ref, ylru_ref, yfox_ref, p_ref, st_ref, *rest, nt):
    wr = rest[:10]
    y_ref, tail_ref, ext_ref = rest[10:]
    nb = x_ref.shape[0]
    d = x_ref.shape[1] // nt
    d_ff = wr[6].shape[0]
    taps = wr[4].shape[0]
    pad = (taps - 1) * nb

    def fill_prev(cs):
        for j in range(taps - 1):
            ext_ref[j * nb:(j + 1) * nb, :] = st_ref[:, j * d_ff + cs.start:j * d_ff + cs.stop]

    def keep_tail(cs):
        for j in range(taps - 1):
            tail_ref[:, j * d_ff + cs.start:j * d_ff + cs.stop] = ext_ref[(nt + j) * nb:(nt + j + 1) * nb, :]

    y = _ffn_math(_time_major(x_ref, nt), _time_major(ylru_ref, nt), _time_major(yfox_ref, nt),
                  _time_major(p_ref, nt), wr, ext_ref, fill_prev, keep_tail, pad=pad, unit=nb)
    for t in range(nt):
        y_ref[:, t * d:(t + 1) * d] = y[t * nb:(t + 1) * nb, :]


def _ffn_sample(x, ylru, yfox, p, state, weights, *, nt, conv_cols):
    nb = x.shape[0]
    taps = weights[4].shape[0]
    args = [x, ylru, yfox, p, state] + list(weights)
    return pl.pallas_call(
        functools.partial(_ffn_sample_kernel, nt=nt),
        grid=(1,),
        in_specs=[_const_spec(a.shape) for a in args],
        out_specs=[_whole_spec(x.shape), _whole_spec(state.shape)],
        out_shape=[jax.ShapeDtypeStruct(x.shape, F32), jax.ShapeDtypeStruct(state.shape, F32)],
        scratch_shapes=[pltpu.VMEM(((taps - 1 + nt) * nb, conv_cols), F32)],
        compiler_params=pltpu.CompilerParams(dimension_semantics=("arbitrary",),
                                             vmem_limit_bytes=VMEM_LIMIT),
        name="ffn_sample",
    )(*args)


def _block_diag(w):
    nblk, k, j = w.shape
    eye = jnp.eye(nblk, dtype=w.dtype)
    return (w[:, :, None, :] * eye[:, None, :, None]).reshape(nblk * k, nblk * j)


def kernel(x_prompt, x_sample, p_prompt, p_sample, cache_k, cache_v, cache_logf, state_lru_h, state_lru_conv, state_ffn_conv, page_table, g_mix, w_in, lru_conv_w, lru_conv_b, lru_w_a, lru_b_a, lru_w_x, lru_b_x, lru_lambda, fox_b_f, g_lru_out, g_fox_out, w_out, g_ffn, w_up, ffn_conv_w, ffn_conv_b, w_down, w_ple, w_ple_gate, g_final):
    depth = w_in.shape[0]
    assert depth == 1, "single-layer step"
    bp, sp, d = x_prompt.shape
    bs, ts, _ = x_sample.shape
    lw = state_lru_h.shape[-1]
    fw = HEADS * HEAD_DIM
    d_ff = w_down.shape[1]
    n_phys, page = cache_k.shape[1], cache_k.shape[2]
    l = 0
    row = lambda a: a.reshape(1, -1)

    w_main = w_in[l][:, :2 * lw + 3 * fw].astype(BF16)
    w_f = jnp.pad(w_in[l][:, 2 * lw + 3 * fw:], ((0, 0), (0, LANES - HEADS))).astype(BF16)
    b_f = jnp.pad(fox_b_f[l], (0, LANES - HEADS)).reshape(1, LANES)
    lru_w = [lru_conv_w[l], row(lru_conv_b[l]), _block_diag(lru_w_a[l]).astype(BF16), row(lru_b_a[l]),
             _block_diag(lru_w_x[l]).astype(BF16), row(lru_b_x[l]), row(lru_lambda[l]), row(g_lru_out[l])]
    ffn_w = [row(g_fox_out[l]), w_out[l].astype(BF16), row(g_ffn[l]), w_up[l].astype(BF16),
             ffn_conv_w[l], row(ffn_conv_b[l]), w_down[l].astype(BF16), w_ple[l].astype(BF16),
             w_ple_gate[l].astype(BF16), row(g_final)]
    g_mix_r = row(g_mix[l])
    w_kvt = jnp.swapaxes(w_in[l][:, 2 * lw + fw:2 * lw + 3 * fw], 0, 1).astype(BF16)

    xp = x_prompt.reshape(bp * sp, d)
    xr, gr, q_bf, kt_p, vt_p, kt_bf, vt_bf, lf_pad = _inproj_prompt(
        xp, g_mix_r, w_main, w_kvt, w_f, b_f, batch=bp, seq=sp, tm=512)
    lft_p, c_rows, c_t = _cumsum(lf_pad, bp, sp)
    ylru_p, hlast_p = _lru_prompt(xr, gr, *lru_w, batch=bp, seq=sp, tt=512)
    yfox_p = _attn_prompt(q_bf, kt_bf, vt_bf, c_rows, c_t.reshape(bp * HEADS, sp),
                          batch=bp, seq=sp, tq=1024, tk=512)
    y_p, ffn_tail_p = _ffn_prompt(xp, ylru_p, yfox_p, p_prompt[l].reshape(bp * sp, -1), ffn_w,
                                  batch=bp, seq=sp, tm=512, conv_cols=1024)
    to_bshd = lambda a: jnp.transpose(a.reshape(1, bp, HEADS, HEAD_DIM, sp), (0, 1, 4, 2, 3))
    k_p, v_p = to_bshd(kt_p), to_bshd(vt_p)
    logf_p = jnp.transpose(lft_p, (0, 2, 1))[None]

    xs = x_sample.reshape(bs, ts * d)
    xr_s, gr_s, q_s, k_s, v_s, lf_s = _inproj_sample(xs, g_mix_r, w_main, w_f, b_f, nt=ts)
    logf_s = lf_s.reshape(bs, ts, LANES)[:, :, :HEADS]
    ylru_s, hlast_s = _lru_sample(
        xr_s, gr_s, state_lru_h[l], state_lru_conv[l].reshape(bs, -1), *lru_w, nt=ts)
    pages_t = lambda c: jnp.transpose(c, (0, 2, 3, 1)).reshape(n_phys, fw, page)
    yfox_s = _attn_decode(
        page_table, q_s.reshape(bs, ts, fw), k_s.reshape(bs, ts, fw), v_s.reshape(bs, ts, fw),
        jnp.pad(jnp.swapaxes(logf_s, 1, 2), ((0, 0), (0, 0), (0, LANES - ts))),
        pages_t(cache_k[l]), pages_t(cache_v[l]), jnp.swapaxes(cache_logf[l], 1, 2), n_pg=16)
    y_s, ffn_tail_s = _ffn_sample(
        xs, ylru_s, yfox_s.reshape(bs, ts * fw), p_sample[l].reshape(bs, -1),
        state_ffn_conv[l].reshape(bs, -1), ffn_w, nt=ts, conv_cols=512)

    lru_taps = lru_conv_w.shape[1]
    return (y_p.reshape(bp, sp, d), y_s.reshape(bs, ts, d),
            k_p, v_p, logf_p, hlast_p.reshape(1, bp, lw),
            xr.reshape(bp, sp, lw)[None, :, sp - (lru_taps - 1):, :], ffn_tail_p[None],
            k_s.reshape(1, bs, ts, HEADS, HEAD_DIM), v_s.reshape(1, bs, ts, HEADS, HEAD_DIM),
            logf_s[None], hlast_s[None],
            xr_s.reshape(bs, ts, lw)[None, :, ts - (lru_taps - 1):, :],
            ffn_tail_s.reshape(1, bs, -1, d_ff))
```

```python
import functools
import math

import jax
import jax.numpy as jnp
from jax import lax
from jax.experimental import pallas as pl
from jax.experimental.pallas import tpu as pltpu

F32 = jnp.float32
BF16 = jnp.bfloat16

EPS = 1e-6
LRU_C = 8.0
HEAD_DIM = 64
HEADS = 8
LANES = 128
SUBLANES = 8
VMEM_LIMIT = 56 * 1024 * 1024
NEG_INF = float("-inf")
GELU_C = math.sqrt(2.0 / math.pi)
LOG2_E = math.log2(math.e)


def _rms(x, g):
    ms = jnp.mean(x * x, axis=-1, keepdims=True)
    return x * lax.rsqrt(ms + EPS) * g


def _gelu(x):
    return x * (0.5 * (1.0 + jnp.tanh(GELU_C * (x + 0.044715 * (x * x * x)))))


def _softplus(x):
    return jnp.maximum(x, 0.0) + jnp.log1p(jnp.exp(-jnp.abs(x)))


def _dot(a, b):
    return jnp.dot(a, b, preferred_element_type=F32)


def _dot_nt(a, b):
    return lax.dot_general(a, b, (((1,), (1,)), ((), ())), preferred_element_type=F32)


def _const_spec(shape):
    nd = len(shape)
    return pl.BlockSpec(shape, lambda *_: (0,) * nd, pipeline_mode=pl.Buffered(1))


def _whole_spec(shape):
    nd = len(shape)
    return pl.BlockSpec(shape, lambda *_: (0,) * nd)


def _time_major(ref, nt):
    c = ref.shape[1] // nt
    return jnp.concatenate([ref[:, t * c:(t + 1) * c] for t in range(nt)], axis=0)


def _inproj_sample_kernel(x_ref, g_ref, wm_ref, wf_ref, bf_ref,
                          xr_ref, gr_ref, q_ref, k_ref, v_ref, lf_ref, *, nt):
    d = x_ref.shape[1] // nt
    w = xr_ref.shape[1] // nt
    for t in range(nt):
        n = _rms(x_ref[:, t * d:(t + 1) * d], g_ref[...]).astype(BF16)
        ws = slice(t * w, (t + 1) * w)
        xr_ref[:, ws] = _dot(n, wm_ref[:, 0 * w:1 * w])
        gr_ref[:, ws] = _dot(n, wm_ref[:, 1 * w:2 * w])
        q_ref[:, ws] = _dot(n, wm_ref[:, 2 * w:3 * w]) * (HEAD_DIM ** -0.5)
        k_ref[:, ws] = _dot(n, wm_ref[:, 3 * w:4 * w])
        v_ref[:, ws] = _dot(n, wm_ref[:, 4 * w:5 * w])
        fl = _dot(n, wf_ref[...]) + bf_ref[...]
        lf_ref[:, t * LANES:(t + 1) * LANES] = -_softplus(-fl)


def _inproj_sample(x, g_mix, w_main, w_f, b_f, *, nt):
    rows = x.shape[0]
    w = w_main.shape[1] // 5
    args = [x, g_mix, w_main, w_f, b_f]
    shapes = [(rows, nt * w)] * 5 + [(rows, nt * LANES)]
    return pl.pallas_call(
        functools.partial(_inproj_sample_kernel, nt=nt),
        grid=(1,),
        in_specs=[_const_spec(a.shape) for a in args],
        out_specs=[_whole_spec(s) for s in shapes],
        out_shape=[jax.ShapeDtypeStruct(s, F32) for s in shapes],
        compiler_params=pltpu.CompilerParams(dimension_semantics=("arbitrary",),
                                             vmem_limit_bytes=VMEM_LIMIT),
        name="inproj_sample",
    )(*args)


def _inproj_prompt_kernel(x_ref, g_ref, wm_ref, wkvt_ref, wf_ref, bf_ref,
                          xr_ref, gr_ref, q_ref, kt_ref, vt_ref, ktb_ref, vtb_ref, lf_ref):
    n = _rms(x_ref[...], g_ref[...]).astype(BF16)
    w = xr_ref.shape[1]
    xr_ref[...] = _dot(n, wm_ref[:, 0 * w:1 * w])
    gr_ref[...] = _dot(n, wm_ref[:, 1 * w:2 * w])
    q_ref[...] = (_dot(n, wm_ref[:, 2 * w:3 * w]) * (HEAD_DIM ** -0.5 * LOG2_E)).astype(q_ref.dtype)
    kt = _dot_nt(wkvt_ref[0:w, :], n)
    vt = _dot_nt(wkvt_ref[w:2 * w, :], n)
    kt_ref[...] = kt
    vt_ref[...] = vt
    ktb_ref[...] = kt.astype(BF16)
    vtb_ref[...] = vt.astype(BF16)
    fl = _dot(n, wf_ref[...]) + bf_ref[...]
    lf_ref[...] = -_softplus(-fl)


def _inproj_prompt(x, g_mix, w_main, w_kvt, w_f, b_f, *, batch, seq, tm):
    rows, d = x.shape
    w = w_kvt.shape[0] // 2
    nt = seq // tm
    row_spec = lambda c: pl.BlockSpec((tm, c), lambda b, t: (b * nt + t, 0))
    col_spec = pl.BlockSpec((w, tm), lambda b, t: (b, t))
    consts = [g_mix, w_main, w_kvt, w_f, b_f]
    return pl.pallas_call(
        _inproj_prompt_kernel,
        grid=(batch, nt),
        in_specs=[row_spec(d)] + [_const_spec(c.shape) for c in consts],
        out_specs=[row_spec(w), row_spec(w), row_spec(w), col_spec, col_spec, col_spec, col_spec,
                   row_spec(LANES)],
        out_shape=[jax.ShapeDtypeStruct((rows, w), F32), jax.ShapeDtypeStruct((rows, w), F32),
                   jax.ShapeDtypeStruct((rows, w), BF16),
                   jax.ShapeDtypeStruct((batch * w, seq), F32), jax.ShapeDtypeStruct((batch * w, seq), F32),
                   jax.ShapeDtypeStruct((batch * w, seq), BF16), jax.ShapeDtypeStruct((batch * w, seq), BF16),
                   jax.ShapeDtypeStruct((rows, LANES), F32)],
        compiler_params=pltpu.CompilerParams(dimension_semantics=("arbitrary", "arbitrary"),
                                             vmem_limit_bytes=VMEM_LIMIT),
        name="inproj_prompt",
    )(x, *consts)


def _cumsum_kernel(lf_ref, lft_ref, c_ref, ct_ref):
    x = lf_ref[...]
    lft_ref[0] = x.T[:HEADS, :]
    n = x.shape[0]
    row = lax.broadcasted_iota(jnp.int32, x.shape, 0)
    d = 1
    while d < n:
        x = x + jnp.where(row >= d, pltpu.roll(x, d, axis=0), 0.0)
        d *= 2
    x = x * LOG2_E
    c_ref[...] = x
    ct_ref[0] = x.T[:HEADS, :]


def _cumsum(lf_pad, batch, seq):
    rows = lf_pad.shape[0]
    head_major = pl.BlockSpec((1, HEADS, seq), lambda b: (b, 0, 0))
    return pl.pallas_call(
        _cumsum_kernel,
        grid=(batch,),
        in_specs=[pl.BlockSpec((seq, LANES), lambda b: (b, 0))],
        out_specs=[head_major, pl.BlockSpec((seq, LANES), lambda b: (b, 0)), head_major],
        out_shape=[jax.ShapeDtypeStruct((batch, HEADS, seq), F32),
                   jax.ShapeDtypeStruct((rows, LANES), F32),
                   jax.ShapeDtypeStruct((batch, HEADS, seq), F32)],
        compiler_params=pltpu.CompilerParams(dimension_semantics=("arbitrary",),
                                             vmem_limit_bytes=VMEM_LIMIT),
        name="logf_cumsum",
    )(lf_pad)


def _causal_conv(ext_ref, cw_ref, cb_ref, cols, *, pad, unit, n):
    taps = cw_ref.shape[0]
    out = cb_ref[:, cols]
    for j in range(taps):
        out = out + cw_ref[taps - 1 - j:taps - j, cols] * ext_ref[pl.ds(pad - j * unit, n), :]
    return out


def _lru_gates(xc, wa_ref, ba_ref, wx_ref, bx_ref, lam_ref):
    xcb = xc.astype(BF16)
    r = jax.nn.sigmoid(_dot(xcb, wa_ref[...]) + ba_ref[...])
    i = jax.nn.sigmoid(_dot(xcb, wx_ref[...]) + bx_ref[...])
    log_a = (-LRU_C) * r * _softplus(-lam_ref[...])
    a = jnp.exp(log_a)
    mult = jnp.sqrt(-jnp.tanh(log_a) * (a * a + 1.0))
    return a, mult, i * xc


def _lru_prompt_kernel(xr_ref, gr_ref, cw_ref, cb_ref, wa_ref, ba_ref, wx_ref, bx_ref, lam_ref,
                       g_ref, y_ref, hlast_ref, ext_ref, a_ref, u_ref, hs_ref, h_ref, *, tt):
    t = pl.program_id(1)
    pad = SUBLANES

    @pl.when(t == 0)
    def _():
        ext_ref[0:pad, :] = jnp.zeros((pad, ext_ref.shape[1]), F32)
        h_ref[...] = jnp.zeros(h_ref.shape, F32)

    x = xr_ref[...]
    ext_ref[pad:pad + tt, :] = x
    xc = _causal_conv(ext_ref, cw_ref, cb_ref, slice(None), pad=pad, unit=1, n=tt)
    ext_ref[0:pad, :] = x[tt - pad:tt, :]

    a, mult, ix = _lru_gates(xc, wa_ref, ba_ref, wx_ref, bx_ref, lam_ref)
    row = lax.broadcasted_iota(jnp.int32, (tt, 1), 0)
    mult = jnp.where((row == 0) & (t == 0), 1.0, mult)
    a_ref[...] = a
    u_ref[...] = mult * ix

    row8 = lax.broadcasted_iota(jnp.int32, (SUBLANES, a_ref.shape[1]), 0)

    def group(gi, h):
        r0 = pl.multiple_of(gi * SUBLANES, SUBLANES)
        ag = a_ref[pl.ds(r0, SUBLANES), :]
        ug = u_ref[pl.ds(r0, SUBLANES), :]
        for d in (1, 2, 4):
            a_sh = jnp.where(row8 >= d, pltpu.roll(ag, d, axis=0), 1.0)
            u_sh = jnp.where(row8 >= d, pltpu.roll(ug, d, axis=0), 0.0)
            ug = ag * u_sh + ug
            ag = ag * a_sh
        hs = ag * h + ug
        hs_ref[pl.ds(r0, SUBLANES), :] = hs
        return jnp.broadcast_to(hs[SUBLANES - 1:SUBLANES, :], hs.shape)

    h = lax.fori_loop(0, tt // SUBLANES, group, h_ref[...], unroll=2)
    h_ref[...] = h
    hlast_ref[0] = h[0:1, :]
    y = hs_ref[...] * _gelu(gr_ref[...])
    y_ref[...] = _rms(y, g_ref[...]).astype(y_ref.dtype)


def _lru_prompt(xr, gr, cw, cb, wa, ba, wx, bx, lam, g_out, *, batch, seq, tt):
    rows, w = xr.shape
    nt = seq // tt
    tile = pl.BlockSpec((tt, w), lambda b, t: (b * nt + t, 0))
    consts = [cw, cb, wa, ba, wx, bx, lam, g_out]
    return pl.pallas_call(
        functools.partial(_lru_prompt_kernel, tt=tt),
        grid=(batch, nt),
        in_specs=[tile, tile] + [_const_spec(c.shape) for c in consts],
        out_specs=[tile, pl.BlockSpec((1, 1, w), lambda b, t: (b, 0, 0))],
        out_shape=[jax.ShapeDtypeStruct((rows, w), BF16), jax.ShapeDtypeStruct((batch, 1, w), F32)],
        scratch_shapes=[pltpu.VMEM((SUBLANES + tt, w), F32), pltpu.VMEM((tt, w), F32),
                        pltpu.VMEM((tt, w), F32), pltpu.VMEM((tt, w), F32),
                        pltpu.VMEM((SUBLANES, w), F32)],
        compiler_params=pltpu.CompilerParams(dimension_semantics=("arbitrary", "arbitrary"),
                                             vmem_limit_bytes=VMEM_LIMIT),
        name="lru_prompt",
    )(xr, gr, *consts)


def _lru_sample_kernel(xr_ref, gr_ref, h0_ref, cbuf_ref, cw_ref, cb_ref, wa_ref, ba_ref, wx_ref,
                       bx_ref, lam_ref, g_ref, y_ref, hlast_ref, ext_ref, *, nt):
    nb, w = h0_ref.shape
    taps = cw_ref.shape[0]
    pad = (taps - 1) * nb
    n = nb * nt
    ext_ref[0:pad, :] = _time_major(cbuf_ref, taps - 1)
    ext_ref[pad:pad + n, :] = _time_major(xr_ref, nt)
    xc = _causal_conv(ext_ref, cw_ref, cb_ref, slice(None), pad=pad, unit=nb, n=n)
    a, mult, ix = _lru_gates(xc, wa_ref, ba_ref, wx_ref, bx_ref, lam_ref)
    u = mult * ix
    h = h0_ref[...]
    for t in range(nt):
        h = a[t * nb:(t + 1) * nb, :] * h + u[t * nb:(t + 1) * nb, :]
        y = h * _gelu(gr_ref[:, t * w:(t + 1) * w])
        y_ref[:, t * w:(t + 1) * w] = _rms(y, g_ref[...]).astype(y_ref.dtype)
    hlast_ref[...] = h


def _lru_sample(xr, gr, h0, cbuf, cw, cb, wa, ba, wx, bx, lam, g_out, *, nt):
    nb, w = h0.shape
    args = [xr, gr, h0, cbuf, cw, cb, wa, ba, wx, bx, lam, g_out]
    taps = cw.shape[0]
    return pl.pallas_call(
        functools.partial(_lru_sample_kernel, nt=nt),
        grid=(1,),
        in_specs=[_const_spec(a.shape) for a in args],
        out_specs=[_whole_spec((nb, nt * w)), _whole_spec((nb, w))],
        out_shape=[jax.ShapeDtypeStruct((nb, nt * w), BF16), jax.ShapeDtypeStruct((nb, w), F32)],
        scratch_shapes=[pltpu.VMEM(((taps - 1 + nt) * nb, w), F32)],
        compiler_params=pltpu.CompilerParams(dimension_semantics=("arbitrary",),
                                             vmem_limit_bytes=VMEM_LIMIT),
        name="lru_sample",
    )(*args)


def _attn_prompt_tile(q_ref, kt_ref, vt_ref, c_ref, ct_ref, o_ref, ck_ref, hp, qi, *, tq, tk):
    heads_per_blk = LANES // HEAD_DIM
    q = q_ref[...]
    c_blk = c_ref[...]
    lane = lax.broadcasted_iota(jnp.int32, (tq, LANES), 1)
    n_full = (qi * tq) // tk
    n_diag = pl.cdiv(tq, tk)
    outs = []
    for hh in range(heads_per_blk):
        h = hp * heads_per_blk + hh
        in_head = (lane >= hh * HEAD_DIM) & (lane < (hh + 1) * HEAD_DIM)
        qm = jnp.where(in_head, q, jnp.zeros_like(q))
        cq = jnp.sum(jnp.where(lane == h, c_blk, 0.0), axis=-1, keepdims=True)
        ck_ref[...] = ct_ref[pl.ds(h, 1), :]

        def tile(j, carry, masked):
            m, l, acc = carry
            ks = pl.multiple_of(j * tk, tk)
            z = _dot(qm, kt_ref[:, pl.ds(ks, tk)]) - ck_ref[:, pl.ds(ks, tk)]
            if masked:
                row_i = lax.broadcasted_iota(jnp.int32, (tq, tk), 0) + qi * tq
                col_i = lax.broadcasted_iota(jnp.int32, (tq, tk), 1) + j * tk
                z = jnp.where(col_i <= row_i, z, NEG_INF)
            m_new = jnp.maximum(m, jnp.max(z, axis=-1, keepdims=True) + cq)
            p = jnp.exp2(z - (m_new - cq))
            alpha = jnp.exp2(m - m_new)
            l = alpha * l + jnp.sum(p, axis=-1, keepdims=True)
            acc = alpha * acc + _dot_nt(p.astype(BF16), vt_ref[:, pl.ds(ks, tk)])
            return m_new, l, acc

        carry = (jnp.full((tq, 1), NEG_INF, F32), jnp.zeros((tq, 1), F32), jnp.zeros((tq, LANES), F32))
        carry = lax.fori_loop(0, n_full, functools.partial(tile, masked=False), carry)
        for dj in range(n_diag):
            carry = tile(n_full + dj, carry, True)
        m, l, acc = carry
        outs.append((in_head, acc / l))
    o_ref[...] = jnp.where(outs[0][0], outs[0][1], outs[1][1])


def _decode_chunk(j, seq, n_seq, pt_ref, q_ref, kn_ref, vn_ref, lfn_ref, ck_hbm, cv_hbm, clf_hbm, o_ref,
                  kbuf, vbuf, lfbuf, sem, qe_ref, cn_ref, cnl_ref, m_ref, l_ref, acc_ref,
                  carry_ref, knp_ref, vnp_ref, *, n_pg, n_chunks, n_slots, nt):
    def chunk_copies(s, jj):
        buf_slot = (s * n_chunks + jj) % n_slots
        first_page = (n_chunks - 1 - jj) * n_pg
        copies = []
        for i in range(n_pg):
            pg = pt_ref[s, first_page + i]
            copies.append(pltpu.make_async_copy(ck_hbm.at[pg], kbuf.at[buf_slot, i], sem.at[0, buf_slot]))
            copies.append(pltpu.make_async_copy(cv_hbm.at[pg], vbuf.at[buf_slot, i], sem.at[1, buf_slot]))
            copies.append(pltpu.make_async_copy(clf_hbm.at[pg], lfbuf.at[buf_slot, i], sem.at[2, buf_slot]))
        return copies

    if j == 0:
        @pl.when(seq == 0)
        def _():
            for jj in range(n_slots):
                for c in chunk_copies(0, jj):
                    c.start()

    for c in chunk_copies(seq, j):
        c.wait()
    slot = (seq * n_chunks + j) % n_slots
    k_refs = [kbuf.at[slot, i] for i in range(n_pg)]
    v_refs = [vbuf.at[slot, i] for i in range(n_pg)]
    lf_refs = [lfbuf.at[slot, i] for i in range(n_pg)]
    w = q_ref.shape[2]
    n_keys = n_pg * LANES
    rows = nt * HEADS
    lane_p = lax.broadcasted_iota(jnp.int32, (HEADS, LANES), 1)
    sub = lax.broadcasted_iota(jnp.int32, (HEADS, w), 0)
    lane_w = lax.broadcasted_iota(jnp.int32, (HEADS, w), 1)
    own_lanes = (lane_w >= sub * HEAD_DIM) & (lane_w < (sub + 1) * HEAD_DIM)

    if j == 0:
        q = q_ref[0]
        lfn = lfn_ref[0]
        cn = jnp.zeros((HEADS, 1), F32)
        cn_lanes = jnp.zeros((HEADS, LANES), F32)
        for t in range(nt):
            qe_ref[t * HEADS:(t + 1) * HEADS, :] = jnp.where(
                own_lanes, jnp.broadcast_to(q[t:t + 1, :], (HEADS, w)), 0.0)
            cn = cn + jnp.sum(jnp.where(lane_p == t, lfn, 0.0), axis=1, keepdims=True)
            cn_ref[t * HEADS:(t + 1) * HEADS, :] = cn
            cn_lanes = jnp.where(lane_p == t, cn, cn_lanes)
        cnl_ref[...] = cn_lanes
        m_ref[...] = jnp.full(m_ref.shape, NEG_INF, F32)
        l_ref[...] = jnp.zeros(l_ref.shape, F32)
        acc_ref[...] = jnp.zeros(acc_ref.shape, F32)
        carry_ref[...] = jnp.zeros(carry_ref.shape, F32)
        knp_ref[...] = jnp.zeros(knp_ref.shape, F32)
        vnp_ref[...] = jnp.zeros(vnp_ref.shape, F32)
        knp_ref[0:nt, :] = kn_ref[0]
        vnp_ref[0:nt, :] = vn_ref[0]

    qe = qe_ref[...].astype(BF16)
    cn = cn_ref[...]

    def online_update(z, pv):
        m_old = m_ref[...]
        m_new = jnp.maximum(m_old, jnp.max(z, axis=-1, keepdims=True) + cn)
        p = jnp.exp(z - (m_new - cn))
        alpha = jnp.exp(m_old - m_new)
        l_ref[...] = alpha * l_ref[...] + jnp.sum(p, axis=-1, keepdims=True)
        acc_ref[...] = alpha * acc_ref[...] + pv(p.astype(BF16))
        m_ref[...] = m_new

    lf = jnp.concatenate([r[...] for r in lf_refs], axis=1)
    lane_k = lax.broadcasted_iota(jnp.int32, lf.shape, 1)
    suf = lf
    d = 1
    while d < n_keys:
        suf = suf + jnp.where(lane_k < n_keys - d, pltpu.roll(suf, n_keys - d, axis=1), 0.0)
        d *= 2
    carry = carry_ref[:, 0:1]
    rc = (suf - lf) + carry
    carry_ref[...] = jnp.broadcast_to(carry + suf[:, 0:1], carry_ref.shape)
    kt_bf = jnp.concatenate([r[...].astype(BF16) for r in k_refs], axis=1)
    vt_bf = jnp.concatenate([r[...].astype(BF16) for r in v_refs], axis=1)
    z = _dot(qe, kt_bf) + jnp.concatenate([rc] * nt, axis=0)
    online_update(z, lambda p: _dot_nt(p, vt_bf))

    nxt_seq_off, nxt_j = divmod(j + n_slots, n_chunks)
    if nxt_seq_off == 0:
        for c in chunk_copies(seq, nxt_j):
            c.start()
    else:
        @pl.when(seq + nxt_seq_off < n_seq)
        def _():
            for c in chunk_copies(seq + nxt_seq_off, nxt_j):
                c.start()

    if j == n_chunks - 1:
        z_new = _dot_nt(qe, knp_ref[...].astype(BF16)) - jnp.concatenate([cnl_ref[...]] * nt, axis=0)
        r_id = lax.broadcasted_iota(jnp.int32, (rows, LANES), 0)
        c_id = lax.broadcasted_iota(jnp.int32, (rows, LANES), 1)
        z_new = jnp.where(c_id * HEADS <= r_id, z_new, NEG_INF)
        online_update(z_new, lambda p: _dot(p, vnp_ref[...].astype(BF16)))
        o = acc_ref[...] / l_ref[...]
        for t in range(nt):
            o_t = jnp.where(own_lanes, o[t * HEADS:(t + 1) * HEADS, :], 0.0)
            o_ref[0, t:t + 1, :] = jnp.sum(o_t, axis=0, keepdims=True)


def _attn_fused_kernel(pt_ref, q_ref, kt_ref, vt_ref, c_ref, ct_ref, qs_ref, kn_ref, vn_ref, lfn_ref,
                       ck_hbm, cv_hbm, clf_hbm, o_ref, os_ref, ck_ref, *decode_scratch,
                       tq, tk, n_pg, n_chunks, n_slots, nt):
    hp = pl.program_id(1)
    qi = pl.program_id(2)
    seq = (pl.program_id(0) * pl.num_programs(1) + hp) * pl.num_programs(2) + qi
    n_seq = pl.num_programs(0) * pl.num_programs(1) * pl.num_programs(2)
    for j in range(n_chunks):
        _decode_chunk(j, seq, n_seq, pt_ref, qs_ref, kn_ref, vn_ref, lfn_ref, ck_hbm, cv_hbm, clf_hbm,
                      os_ref, *decode_scratch, n_pg=n_pg, n_chunks=n_chunks, n_slots=n_slots, nt=nt)
    _attn_prompt_tile(q_ref, kt_ref, vt_ref, c_ref, ct_ref, o_ref, ck_ref, hp, qi, tq=tq, tk=tk)


def _attn_fused(q_bf, kt_bf, vt_bf, c_rows, c_t, page_table, q_s, k_new, v_new, lf_new_t,
                cache_k, cache_v, cache_lf_t, *, batch, seq, tq, tk, n_pg, n_slots):
    rows, w = q_bf.shape
    nq = seq // tq
    n_hp = w // LANES
    nb, nt, _ = q_s.shape
    n_chunks = page_table.shape[1] // n_pg
    page = cache_k.shape[2]
    assert page == LANES and cache_k.shape[1] == w
    assert nb == batch * n_hp * nq, "one sample sequence per prompt attention grid step"
    assert n_slots <= n_chunks
    srows = nt * HEADS

    def seq_spec(shape):
        return pl.BlockSpec((1,) + shape, lambda b, hp, qi, pt: ((b * n_hp + hp) * nq + qi, 0, 0))

    kv_spec = pl.BlockSpec((LANES, seq), lambda b, hp, qi, pt: (b * n_hp + hp, 0))
    q_spec = pl.BlockSpec((tq, LANES), lambda b, hp, qi, pt: (b * nq + qi, hp))
    in_specs = ([q_spec, kv_spec, kv_spec,
                 pl.BlockSpec((tq, LANES), lambda b, hp, qi, pt: (b * nq + qi, 0)),
                 pl.BlockSpec((HEADS, seq), lambda b, hp, qi, pt: (b, 0))]
                + [seq_spec((nt, w)) for _ in range(3)] + [seq_spec((HEADS, LANES))]
                + [pl.BlockSpec(memory_space=pl.ANY) for _ in range(3)])
    grid_spec = pltpu.PrefetchScalarGridSpec(
        num_scalar_prefetch=1, grid=(batch, n_hp, nq), in_specs=in_specs,
        out_specs=[q_spec, seq_spec((nt, w))],
        scratch_shapes=[pltpu.VMEM((1, seq), F32),
                        pltpu.VMEM((n_slots, n_pg, w, page), F32), pltpu.VMEM((n_slots, n_pg, w, page), F32),
                        pltpu.VMEM((n_slots, n_pg, HEADS, page), F32), pltpu.SemaphoreType.DMA((3, n_slots)),
                        pltpu.VMEM((srows, w), F32), pltpu.VMEM((srows, 1), F32),
                        pltpu.VMEM((HEADS, LANES), F32),
                        pltpu.VMEM((srows, 1), F32), pltpu.VMEM((srows, 1), F32),
                        pltpu.VMEM((srows, w), F32), pltpu.VMEM((HEADS, LANES), F32),
                        pltpu.VMEM((LANES, w), F32), pltpu.VMEM((LANES, w), F32)])
    return pl.pallas_call(
        functools.partial(_attn_fused_kernel, tq=tq, tk=tk, n_pg=n_pg, n_chunks=n_chunks,
                          n_slots=n_slots, nt=nt),
        grid_spec=grid_spec,
        out_shape=[jax.ShapeDtypeStruct((rows, w), F32), jax.ShapeDtypeStruct((nb, nt, w), F32)],
        compiler_params=pltpu.CompilerParams(
            dimension_semantics=("arbitrary", "arbitrary", "arbitrary"),
            vmem_limit_bytes=VMEM_LIMIT),
        name="attn_fused",
    )(page_table, q_bf, kt_bf, vt_bf, c_rows, c_t, q_s, k_new, v_new, lf_new_t,
      cache_k, cache_v, cache_lf_t)


def _ffn_math(x, ylru_bf, yfox, p, wr, ext_ref, fill_prev, keep_tail, *, pad, unit):
    (gfox_ref, wout_ref, gffn_ref, wup_ref, cw_ref, cb_ref, wdown_ref, wple_ref, wgate_ref,
     gfin_ref) = wr
    n = x.shape[0]
    wl = ylru_bf.shape[1]
    yf = _rms(yfox, gfox_ref[...]).astype(BF16)
    h = x + _dot(ylru_bf, wout_ref[0:wl, :]) + _dot(yf, wout_ref[wl:, :])
    n2 = _rms(h, gffn_ref[...]).astype(BF16)
    d_ff = wdown_ref.shape[0]
    cw = ext_ref.shape[1]
    acc = jnp.zeros(x.shape, F32)
    for c in range(d_ff // cw):
        cs = slice(c * cw, (c + 1) * cw)
        g = _dot(n2, wup_ref[:, cs])
        u = _dot(n2, wup_ref[:, d_ff + c * cw:d_ff + (c + 1) * cw])
        fill_prev(cs)
        ext_ref[pad:pad + n, :] = g
        gc = _causal_conv(ext_ref, cw_ref, cb_ref, cs, pad=pad, unit=unit, n=n)
        keep_tail(cs)
        acc = acc + _dot((_gelu(gc) * u).astype(BF16), wdown_ref[cs, :])
    h = h + acc
    e = _dot(p.astype(BF16), wple_ref[...]) * jax.nn.sigmoid(_dot(h.astype(BF16), wgate_ref[...]))
    return _rms(h + e, gfin_ref[...])


def _ffn_prompt_kernel(x_ref, ylru_ref, yfox_ref, p_ref, *rest, tm):
    wr = rest[:10]
    y_ref, tail_ref, ext_ref, carry_ref = rest[10:]
    t = pl.program_id(1)
    pad = SUBLANES
    taps = wr[4].shape[0]

    @pl.when(t == 0)
    def _():
        carry_ref[...] = jnp.zeros(carry_ref.shape, F32)

    def fill_prev(cs):
        ext_ref[0:pad, :] = carry_ref[:, cs]

    def keep_tail(cs):
        carry_ref[:, cs] = ext_ref[tm:tm + pad, :]
        tail_ref[0, :, cs] = ext_ref[pad + tm - (taps - 1):pad + tm, :]

    y_ref[...] = _ffn_math(x_ref[...], ylru_ref[...], yfox_ref[...], p_ref[...], wr, ext_ref,
                           fill_prev, keep_tail, pad=pad, unit=1)


def _ffn_prompt(x, ylru, yfox, p, weights, *, batch, seq, tm, conv_cols):
    rows, d = x.shape
    nt = seq // tm
    d_ff = weights[6].shape[0]
    taps = weights[4].shape[0]
    tile = lambda c: pl.BlockSpec((tm, c), lambda b, t: (b * nt + t, 0))
    return pl.pallas_call(
        functools.partial(_ffn_prompt_kernel, tm=tm),
        grid=(batch, nt),
        in_specs=[tile(d), tile(ylru.shape[1]), tile(yfox.shape[1]), tile(p.shape[1])]
        + [_const_spec(wt.shape) for wt in weights],
        out_specs=[tile(d), pl.BlockSpec((1, taps - 1, d_ff), lambda b, t: (b, 0, 0))],
        out_shape=[jax.ShapeDtypeStruct((rows, d), F32),
                   jax.ShapeDtypeStruct((batch, taps - 1, d_ff), F32)],
        scratch_shapes=[pltpu.VMEM((SUBLANES + tm, conv_cols), F32),
                        pltpu.VMEM((SUBLANES, d_ff), F32)],
        compiler_params=pltpu.CompilerParams(dimension_semantics=("arbitrary", "arbitrary"),
                                             vmem_limit_bytes=VMEM_LIMIT),
        name="ffn_prompt",
    )(x, ylru, yfox, p, *weights)


def _ffn_sample_kernel(x_ref, ylru_ref, yfox_ref, p_ref, st_ref, *rest, nt):
    wr = rest[:10]
    y_ref, tail_ref, ext_ref = rest[10:]
    nb = x_ref.shape[0]
    d = x_ref.shape[1] // nt
    d_ff = wr[6].shape[0]
    taps = wr[4].shape[0]
    pad = (taps - 1) * nb

    def fill_prev(cs):
        for j in range(taps - 1):
            ext_ref[j * nb:(j + 1) * nb, :] = st_ref[:, j * d_ff + cs.start:j * d_ff + cs.stop]

    def keep_tail(cs):
        for j in range(taps - 1):
            tail_ref[:, j * d_ff + cs.start:j * d_ff + cs.stop] = ext_ref[(nt + j) * nb:(nt + j + 1) * nb, :]

    y = _ffn_math(_time_major(x_ref, nt), _time_major(ylru_ref, nt), _time_major(yfox_ref, nt),
                  _time_major(p_ref, nt), wr, ext_ref, fill_prev, keep_tail, pad=pad, unit=nb)
    for t in range(nt):
        y_ref[:, t * d:(t + 1) * d] = y[t * nb:(t + 1) * nb, :]


def _ffn_sample(x, ylru, yfox, p, state, weights, *, nt, conv_cols):
    nb = x.shape[0]
    taps = weights[4].shape[0]
    args = [x, ylru, yfox, p, state] + list(weights)
    return pl.pallas_call(
        functools.partial(_ffn_sample_kernel, nt=nt),
        grid=(1,),
        in_specs=[_const_spec(a.shape) for a in args],
        out_specs=[_whole_spec(x.shape), _whole_spec(state.shape)],
        out_shape=[jax.ShapeDtypeStruct(x.shape, F32), jax.ShapeDtypeStruct(state.shape, F32)],
        scratch_shapes=[pltpu.VMEM(((taps - 1 + nt) * nb, conv_cols), F32)],
        compiler_params=pltpu.CompilerParams(dimension_semantics=("arbitrary",),
                                             vmem_limit_bytes=VMEM_LIMIT),
        name="ffn_sample",
    )(*args)


def _block_diag(w):
    nblk, k, j = w.shape
    eye = jnp.eye(nblk, dtype=w.dtype)
    return (w[:, :, None, :] * eye[:, None, :, None]).reshape(nblk * k, nblk * j)


def kernel(x_prompt, x_sample, p_prompt, p_sample, cache_k, cache_v, cache_logf, state_lru_h, state_lru_conv, state_ffn_conv, page_table, g_mix, w_in, lru_conv_w, lru_conv_b, lru_w_a, lru_b_a, lru_w_x, lru_b_x, lru_lambda, fox_b_f, g_lru_out, g_fox_out, w_out, g_ffn, w_up, ffn_conv_w, ffn_conv_b, w_down, w_ple, w_ple_gate, g_final):
    depth = w_in.shape[0]
    assert depth == 1, "single-layer step"
    bp, sp, d = x_prompt.shape
    bs, ts, _ = x_sample.shape
    lw = state_lru_h.shape[-1]
    fw = HEADS * HEAD_DIM
    d_ff = w_down.shape[1]
    n_phys, page = cache_k.shape[1], cache_k.shape[2]
    l = 0
    row = lambda a: a.reshape(1, -1)

    w_main = w_in[l][:, :2 * lw + 3 * fw].astype(BF16)
    w_f = jnp.pad(w_in[l][:, 2 * lw + 3 * fw:], ((0, 0), (0, LANES - HEADS))).astype(BF16)
    b_f = jnp.pad(fox_b_f[l], (0, LANES - HEADS)).reshape(1, LANES)
    lru_w = [lru_conv_w[l], row(lru_conv_b[l]), _block_diag(lru_w_a[l]).astype(BF16), row(lru_b_a[l]),
             _block_diag(lru_w_x[l]).astype(BF16), row(lru_b_x[l]), row(lru_lambda[l]), row(g_lru_out[l])]
    ffn_w = [row(g_fox_out[l]), w_out[l].astype(BF16), row(g_ffn[l]), w_up[l].astype(BF16),
             ffn_conv_w[l], row(ffn_conv_b[l]), w_down[l].astype(BF16), w_ple[l].astype(BF16),
             w_ple_gate[l].astype(BF16), row(g_final)]
    g_mix_r = row(g_mix[l])
    w_kvt = jnp.swapaxes(w_in[l][:, 2 * lw + fw:2 * lw + 3 * fw], 0, 1).astype(BF16)

    xp = x_prompt.reshape(bp * sp, d)
    xr, gr, q_bf, kt_p, vt_p, kt_bf, vt_bf, lf_pad = _inproj_prompt(
        xp, g_mix_r, w_main, w_kvt, w_f, b_f, batch=bp, seq=sp, tm=512)
    lft_p, c_rows, c_t = _cumsum(lf_pad, bp, sp)
    ylru_p, hlast_p = _lru_prompt(xr, gr, *lru_w, batch=bp, seq=sp, tt=512)
    to_bshd = lambda a: jnp.transpose(a.reshape(1, bp, HEADS, HEAD_DIM, sp), (0, 1, 4, 2, 3))
    k_p, v_p = to_bshd(kt_p), to_bshd(vt_p)
    logf_p = jnp.transpose(lft_p, (0, 2, 1))[None]

    xs = x_sample.reshape(bs, ts * d)
    xr_s, gr_s, q_s, k_s, v_s, lf_s = _inproj_sample(xs, g_mix_r, w_main, w_f, b_f, nt=ts)
    logf_s = lf_s.reshape(bs, ts, LANES)[:, :, :HEADS]
    ylru_s, hlast_s = _lru_sample(
        xr_s, gr_s, state_lru_h[l], state_lru_conv[l].reshape(bs, -1), *lru_w, nt=ts)

    pages_t = lambda c: jnp.transpose(c, (0, 2, 3, 1)).reshape(n_phys, fw, page)
    yfox_p, yfox_s = _attn_fused(
        q_bf, kt_bf, vt_bf, c_rows, c_t.reshape(bp * HEADS, sp),
        page_table, q_s.reshape(bs, ts, fw), k_s.reshape(bs, ts, fw), v_s.reshape(bs, ts, fw),
        jnp.pad(jnp.swapaxes(logf_s, 1, 2), ((0, 0), (0, 0), (0, LANES - ts))),
        pages_t(cache_k[l]), pages_t(cache_v[l]), jnp.swapaxes(cache_logf[l], 1, 2),
        batch=bp, seq=sp, tq=1024, tk=512, n_pg=16, n_slots=3)
    y_p, ffn_tail_p = _ffn_prompt(xp, ylru_p, yfox_p, p_prompt[l].reshape(bp * sp, -1), ffn_w,
                                  batch=bp, seq=sp, tm=512, conv_cols=1024)
    y_s, ffn_tail_s = _ffn_sample(
        xs, ylru_s, yfox_s.reshape(bs, ts * fw), p_sample[l].reshape(bs, -1),
        state_ffn_conv[l].reshape(bs, -1), ffn_w, nt=ts, conv_cols=512)

    lru_taps = lru_conv_w.shape[1]
    return (y_p.reshape(bp, sp, d), y_s.reshape(bs, ts, d),
            k_p, v_p, logf_p, hlast_p.reshape(1, bp, lw),
            xr.reshape(bp, sp, lw)[None, :, sp - (lru_taps - 1):, :], ffn_tail_p[None],
            k_s.reshape(1, bs, ts, HEADS, HEAD_DIM), v_s.reshape(1, bs, ts, HEADS, HEAD_DIM),
            logf_s[None], hlast_s[None],
            xr_s.reshape(bs, ts, lw)[None, :, ts - (lru_taps - 1):, :],
            ffn_tail_s.reshape(1, bs, -1, d_ff))
```

```python
import functools
import math

import jax
import jax.numpy as jnp
from jax import lax
from jax.experimental import pallas as pl
from jax.experimental.pallas import tpu as pltpu

F32 = jnp.float32
BF16 = jnp.bfloat16

EPS = 1e-6
LRU_C = 8.0
HEAD_DIM = 64
HEADS = 8
LANES = 128
SUBLANES = 8
VMEM_LIMIT = 56 * 1024 * 1024
NEG_INF = float("-inf")
GELU_C = math.sqrt(2.0 / math.pi)
LOG2_E = math.log2(math.e)


def _rms(x, g):
    ms = jnp.mean(x * x, axis=-1, keepdims=True)
    return x * lax.rsqrt(ms + EPS) * g


def _gelu(x):
    return x * (0.5 * (1.0 + jnp.tanh(GELU_C * (x + 0.044715 * (x * x * x)))))


def _softplus(x):
    return jnp.maximum(x, 0.0) + jnp.log1p(jnp.exp(-jnp.abs(x)))


def _dot(a, b):
    return jnp.dot(a, b, preferred_element_type=F32)


def _dot_nt(a, b):
    return lax.dot_general(a, b, (((1,), (1,)), ((), ())), preferred_element_type=F32)


def _const_spec(shape):
    nd = len(shape)
    return pl.BlockSpec(shape, lambda *_: (0,) * nd, pipeline_mode=pl.Buffered(1))


def _whole_spec(shape):
    nd = len(shape)
    return pl.BlockSpec(shape, lambda *_: (0,) * nd)


def _time_major(ref, nt):
    c = ref.shape[1] // nt
    return jnp.concatenate([ref[:, t * c:(t + 1) * c] for t in range(nt)], axis=0)


def _inproj_sample_kernel(x_ref, g_ref, wm_ref, wf_ref, bf_ref,
                          xr_ref, gr_ref, q_ref, k_ref, v_ref, lf_ref, *, nt):
    d = x_ref.shape[1] // nt
    w = xr_ref.shape[1] // nt
    for t in range(nt):
        n = _rms(x_ref[:, t * d:(t + 1) * d], g_ref[...]).astype(BF16)
        ws = slice(t * w, (t + 1) * w)
        xr_ref[:, ws] = _dot(n, wm_ref[:, 0 * w:1 * w])
        gr_ref[:, ws] = _dot(n, wm_ref[:, 1 * w:2 * w])
        q_ref[:, ws] = _dot(n, wm_ref[:, 2 * w:3 * w]) * (HEAD_DIM ** -0.5)
        k_ref[:, ws] = _dot(n, wm_ref[:, 3 * w:4 * w])
        v_ref[:, ws] = _dot(n, wm_ref[:, 4 * w:5 * w])
        fl = _dot(n, wf_ref[...]) + bf_ref[...]
        lf_ref[:, t * LANES:(t + 1) * LANES] = -_softplus(-fl)


def _inproj_sample(x, g_mix, w_main, w_f, b_f, *, nt):
    rows = x.shape[0]
    w = w_main.shape[1] // 5
    args = [x, g_mix, w_main, w_f, b_f]
    shapes = [(rows, nt * w)] * 5 + [(rows, nt * LANES)]
    return pl.pallas_call(
        functools.partial(_inproj_sample_kernel, nt=nt),
        grid=(1,),
        in_specs=[_const_spec(a.shape) for a in args],
        out_specs=[_whole_spec(s) for s in shapes],
        out_shape=[jax.ShapeDtypeStruct(s, F32) for s in shapes],
        compiler_params=pltpu.CompilerParams(dimension_semantics=("arbitrary",),
                                             vmem_limit_bytes=VMEM_LIMIT),
        name="inproj_sample",
    )(*args)


def _inproj_prompt_kernel(x_ref, g_ref, wm_ref, wkvt_ref, wf_ref, bf_ref,
                          xr_ref, gr_ref, q_ref, kt_ref, vt_ref, ktb_ref, vtb_ref, lf_ref):
    n = _rms(x_ref[...], g_ref[...]).astype(BF16)
    w = xr_ref.shape[1]
    xr_ref[...] = _dot(n, wm_ref[:, 0 * w:1 * w])
    gr_ref[...] = _dot(n, wm_ref[:, 1 * w:2 * w])
    q_ref[...] = (_dot(n, wm_ref[:, 2 * w:3 * w]) * (HEAD_DIM ** -0.5 * LOG2_E)).astype(q_ref.dtype)
    kt = _dot_nt(wkvt_ref[0:w, :], n)
    vt = _dot_nt(wkvt_ref[w:2 * w, :], n)
    kt_ref[...] = kt
    vt_ref[...] = vt
    ktb_ref[...] = kt.astype(BF16)
    vtb_ref[...] = vt.astype(BF16)
    fl = _dot(n, wf_ref[...]) + bf_ref[...]
    lf_ref[...] = -_softplus(-fl)


def _inproj_prompt(x, g_mix, w_main, w_kvt, w_f, b_f, *, batch, seq, tm):
    rows, d = x.shape
    w = w_kvt.shape[0] // 2
    nt = seq // tm
    row_spec = lambda c: pl.BlockSpec((tm, c), lambda b, t: (b * nt + t, 0))
    col_spec = pl.BlockSpec((w, tm), lambda b, t: (b, t))
    consts = [g_mix, w_main, w_kvt, w_f, b_f]
    return pl.pallas_call(
        _inproj_prompt_kernel,
        grid=(batch, nt),
        in_specs=[row_spec(d)] + [_const_spec(c.shape) for c in consts],
        out_specs=[row_spec(w), row_spec(w), row_spec(w), col_spec, col_spec, col_spec, col_spec,
                   row_spec(LANES)],
        out_shape=[jax.ShapeDtypeStruct((rows, w), F32), jax.ShapeDtypeStruct((rows, w), F32),
                   jax.ShapeDtypeStruct((rows, w), BF16),
                   jax.ShapeDtypeStruct((batch * w, seq), F32), jax.ShapeDtypeStruct((batch * w, seq), F32),
                   jax.ShapeDtypeStruct((batch * w, seq), BF16), jax.ShapeDtypeStruct((batch * w, seq), BF16),
                   jax.ShapeDtypeStruct((rows, LANES), F32)],
        compiler_params=pltpu.CompilerParams(dimension_semantics=("arbitrary", "arbitrary"),
                                             vmem_limit_bytes=VMEM_LIMIT),
        name="inproj_prompt",
    )(x, *consts)


def _cumsum_kernel(lf_ref, lft_ref, c_ref, ct_ref):
    x = lf_ref[...]
    lft_ref[0] = x.T[:HEADS, :]
    n = x.shape[0]
    row = lax.broadcasted_iota(jnp.int32, x.shape, 0)
    d = 1
    while d < n:
        x = x + jnp.where(row >= d, pltpu.roll(x, d, axis=0), 0.0)
        d *= 2
    x = x * LOG2_E
    c_ref[...] = x
    ct_ref[0] = x.T[:HEADS, :]


def _cumsum(lf_pad, batch, seq):
    rows = lf_pad.shape[0]
    head_major = pl.BlockSpec((1, HEADS, seq), lambda b: (b, 0, 0))
    return pl.pallas_call(
        _cumsum_kernel,
        grid=(batch,),
        in_specs=[pl.BlockSpec((seq, LANES), lambda b: (b, 0))],
        out_specs=[head_major, pl.BlockSpec((seq, LANES), lambda b: (b, 0)), head_major],
        out_shape=[jax.ShapeDtypeStruct((batch, HEADS, seq), F32),
                   jax.ShapeDtypeStruct((rows, LANES), F32),
                   jax.ShapeDtypeStruct((batch, HEADS, seq), F32)],
        compiler_params=pltpu.CompilerParams(dimension_semantics=("arbitrary",),
                                             vmem_limit_bytes=VMEM_LIMIT),
        name="logf_cumsum",
    )(lf_pad)


def _causal_conv(ext_ref, cw_ref, cb_ref, cols, *, pad, unit, n):
    taps = cw_ref.shape[0]
    out = cb_ref[:, cols]
    for j in range(taps):
        out = out + cw_ref[taps - 1 - j:taps - j, cols] * ext_ref[pl.ds(pad - j * unit, n), :]
    return out


def _lru_gates(xc, wa_ref, ba_ref, wx_ref, bx_ref, lam_ref):
    xcb = xc.astype(BF16)
    r = jax.nn.sigmoid(_dot(xcb, wa_ref[...]) + ba_ref[...])
    i = jax.nn.sigmoid(_dot(xcb, wx_ref[...]) + bx_ref[...])
    log_a = (-LRU_C) * r * _softplus(-lam_ref[...])
    a = jnp.exp(log_a)
    mult = jnp.sqrt(-jnp.tanh(log_a) * (a * a + 1.0))
    return a, mult, i * xc


def _lru_prompt_kernel(xr_ref, gr_ref, cw_ref, cb_ref, wa_ref, ba_ref, wx_ref, bx_ref, lam_ref,
                       g_ref, y_ref, hlast_ref, ext_ref, a_ref, u_ref, hs_ref, h_ref, *, tt):
    t = pl.program_id(1)
    pad = SUBLANES

    @pl.when(t == 0)
    def _():
        ext_ref[0:pad, :] = jnp.zeros((pad, ext_ref.shape[1]), F32)
        h_ref[...] = jnp.zeros(h_ref.shape, F32)

    x = xr_ref[...]
    ext_ref[pad:pad + tt, :] = x
    xc = _causal_conv(ext_ref, cw_ref, cb_ref, slice(None), pad=pad, unit=1, n=tt)
    ext_ref[0:pad, :] = x[tt - pad:tt, :]

    a, mult, ix = _lru_gates(xc, wa_ref, ba_ref, wx_ref, bx_ref, lam_ref)
    row = lax.broadcasted_iota(jnp.int32, (tt, 1), 0)
    mult = jnp.where((row == 0) & (t == 0), 1.0, mult)
    a_ref[...] = a
    u_ref[...] = mult * ix

    row8 = lax.broadcasted_iota(jnp.int32, (SUBLANES, a_ref.shape[1]), 0)

    def group(gi, h):
        r0 = pl.multiple_of(gi * SUBLANES, SUBLANES)
        ag = a_ref[pl.ds(r0, SUBLANES), :]
        ug = u_ref[pl.ds(r0, SUBLANES), :]
        for d in (1, 2, 4):
            a_sh = jnp.where(row8 >= d, pltpu.roll(ag, d, axis=0), 1.0)
            u_sh = jnp.where(row8 >= d, pltpu.roll(ug, d, axis=0), 0.0)
            ug = ag * u_sh + ug
            ag = ag * a_sh
        hs = ag * h + ug
        hs_ref[pl.ds(r0, SUBLANES), :] = hs
        return jnp.broadcast_to(hs[SUBLANES - 1:SUBLANES, :], hs.shape)

    h = lax.fori_loop(0, tt // SUBLANES, group, h_ref[...], unroll=2)
    h_ref[...] = h
    hlast_ref[0] = h[0:1, :]
    y = hs_ref[...] * _gelu(gr_ref[...])
    y_ref[...] = _rms(y, g_ref[...]).astype(y_ref.dtype)


def _lru_prompt(xr, gr, cw, cb, wa, ba, wx, bx, lam, g_out, *, batch, seq, tt):
    rows, w = xr.shape
    nt = seq // tt
    tile = pl.BlockSpec((tt, w), lambda b, t: (b * nt + t, 0))
    consts = [cw, cb, wa, ba, wx, bx, lam, g_out]
    return pl.pallas_call(
        functools.partial(_lru_prompt_kernel, tt=tt),
        grid=(batch, nt),
        in_specs=[tile, tile] + [_const_spec(c.shape) for c in consts],
        out_specs=[tile, pl.BlockSpec((1, 1, w), lambda b, t: (b, 0, 0))],
        out_shape=[jax.ShapeDtypeStruct((rows, w), BF16), jax.ShapeDtypeStruct((batch, 1, w), F32)],
        scratch_shapes=[pltpu.VMEM((SUBLANES + tt, w), F32), pltpu.VMEM((tt, w), F32),
                        pltpu.VMEM((tt, w), F32), pltpu.VMEM((tt, w), F32),
                        pltpu.VMEM((SUBLANES, w), F32)],
        compiler_params=pltpu.CompilerParams(dimension_semantics=("arbitrary", "arbitrary"),
                                             vmem_limit_bytes=VMEM_LIMIT),
        name="lru_prompt",
    )(xr, gr, *consts)


def _lru_sample_kernel(xr_ref, gr_ref, h0_ref, cbuf_ref, cw_ref, cb_ref, wa_ref, ba_ref, wx_ref,
                       bx_ref, lam_ref, g_ref, y_ref, hlast_ref, ext_ref, *, nt):
    nb, w = h0_ref.shape
    taps = cw_ref.shape[0]
    pad = (taps - 1) * nb
    n = nb * nt
    ext_ref[0:pad, :] = _time_major(cbuf_ref, taps - 1)
    ext_ref[pad:pad + n, :] = _time_major(xr_ref, nt)
    xc = _causal_conv(ext_ref, cw_ref, cb_ref, slice(None), pad=pad, unit=nb, n=n)
    a, mult, ix = _lru_gates(xc, wa_ref, ba_ref, wx_ref, bx_ref, lam_ref)
    u = mult * ix
    h = h0_ref[...]
    for t in range(nt):
        h = a[t * nb:(t + 1) * nb, :] * h + u[t * nb:(t + 1) * nb, :]
        y = h * _gelu(gr_ref[:, t * w:(t + 1) * w])
        y_ref[:, t * w:(t + 1) * w] = _rms(y, g_ref[...]).astype(y_ref.dtype)
    hlast_ref[...] = h


def _lru_sample(xr, gr, h0, cbuf, cw, cb, wa, ba, wx, bx, lam, g_out, *, nt):
    nb, w = h0.shape
    args = [xr, gr, h0, cbuf, cw, cb, wa, ba, wx, bx, lam, g_out]
    taps = cw.shape[0]
    return pl.pallas_call(
        functools.partial(_lru_sample_kernel, nt=nt),
        grid=(1,),
        in_specs=[_const_spec(a.shape) for a in args],
        out_specs=[_whole_spec((nb, nt * w)), _whole_spec((nb, w))],
        out_shape=[jax.ShapeDtypeStruct((nb, nt * w), BF16), jax.ShapeDtypeStruct((nb, w), F32)],
        scratch_shapes=[pltpu.VMEM(((taps - 1 + nt) * nb, w), F32)],
        compiler_params=pltpu.CompilerParams(dimension_semantics=("arbitrary",),
                                             vmem_limit_bytes=VMEM_LIMIT),
        name="lru_sample",
    )(*args)


def _attn_prompt_tile(q_ref, kt_ref, vt_ref, c_ref, ct_ref, o_ref, ck_ref, hp, qi, *, tq, tk):
    heads_per_blk = LANES // HEAD_DIM
    q = q_ref[...]
    c_blk = c_ref[...]
    lane = lax.broadcasted_iota(jnp.int32, (tq, LANES), 1)
    n_full = (qi * tq) // tk
    n_diag = pl.cdiv(tq, tk)
    outs = []
    for hh in range(heads_per_blk):
        h = hp * heads_per_blk + hh
        in_head = (lane >= hh * HEAD_DIM) & (lane < (hh + 1) * HEAD_DIM)
        qm = jnp.where(in_head, q, jnp.zeros_like(q))
        cq = jnp.sum(jnp.where(lane == h, c_blk, 0.0), axis=-1, keepdims=True)
        ck_ref[...] = ct_ref[pl.ds(h, 1), :]

        def tile(j, carry, masked):
            m, l, acc = carry
            ks = pl.multiple_of(j * tk, tk)
            z = _dot(qm, kt_ref[:, pl.ds(ks, tk)]) - ck_ref[:, pl.ds(ks, tk)]
            if masked:
                row_i = lax.broadcasted_iota(jnp.int32, (tq, tk), 0) + qi * tq
                col_i = lax.broadcasted_iota(jnp.int32, (tq, tk), 1) + j * tk
                z = jnp.where(col_i <= row_i, z, NEG_INF)
            m_new = jnp.maximum(m, jnp.max(z, axis=-1, keepdims=True) + cq)
            p = jnp.exp2(z - (m_new - cq))
            alpha = jnp.exp2(m - m_new)
            l = alpha * l + jnp.sum(p, axis=-1, keepdims=True)
            acc = alpha * acc + _dot_nt(p.astype(BF16), vt_ref[:, pl.ds(ks, tk)])
            return m_new, l, acc

        carry = (jnp.full((tq, 1), NEG_INF, F32), jnp.zeros((tq, 1), F32), jnp.zeros((tq, LANES), F32))
        carry = lax.fori_loop(0, n_full, functools.partial(tile, masked=False), carry)
        for dj in range(n_diag):
            carry = tile(n_full + dj, carry, True)
        m, l, acc = carry
        outs.append((in_head, acc / l))
    o_ref[...] = jnp.where(outs[0][0], outs[0][1], outs[1][1])


def _decode_chunk(j, seq, n_seq, pt_ref, q_ref, kn_ref, vn_ref, lfn_ref, ck_hbm, cv_hbm, clf_hbm, o_ref,
                  kbuf, vbuf, lfbuf, sem, qe_ref, cn_ref, cnl_ref, m_ref, l_ref, acc_ref,
                  carry_ref, knp_ref, vnp_ref, *, n_pg, n_chunks, n_slots, nt):
    def chunk_copies(s, jj):
        buf_slot = (s * n_chunks + jj) % n_slots
        first_page = (n_chunks - 1 - jj) * n_pg
        copies = []
        for i in range(n_pg):
            pg = pt_ref[s, first_page + i]
            copies.append(pltpu.make_async_copy(ck_hbm.at[pg], kbuf.at[buf_slot, i], sem.at[0, buf_slot]))
            copies.append(pltpu.make_async_copy(cv_hbm.at[pg], vbuf.at[buf_slot, i], sem.at[1, buf_slot]))
            copies.append(pltpu.make_async_copy(clf_hbm.at[pg], lfbuf.at[buf_slot, i], sem.at[2, buf_slot]))
        return copies

    if j == 0:
        @pl.when(seq == 0)
        def _():
            for jj in range(n_slots):
                for c in chunk_copies(0, jj):
                    c.start()

    for c in chunk_copies(seq, j):
        c.wait()
    slot = (seq * n_chunks + j) % n_slots
    k_refs = [kbuf.at[slot, i] for i in range(n_pg)]
    v_refs = [vbuf.at[slot, i] for i in range(n_pg)]
    lf_refs = [lfbuf.at[slot, i] for i in range(n_pg)]
    w = q_ref.shape[2]
    n_keys = n_pg * LANES
    rows = nt * HEADS
    lane_p = lax.broadcasted_iota(jnp.int32, (HEADS, LANES), 1)
    sub = lax.broadcasted_iota(jnp.int32, (HEADS, w), 0)
    lane_w = lax.broadcasted_iota(jnp.int32, (HEADS, w), 1)
    own_lanes = (lane_w >= sub * HEAD_DIM) & (lane_w < (sub + 1) * HEAD_DIM)

    if j == 0:
        q = q_ref[0]
        lfn = lfn_ref[0]
        cn = jnp.zeros((HEADS, 1), F32)
        cn_lanes = jnp.zeros((HEADS, LANES), F32)
        for t in range(nt):
            qe_ref[t * HEADS:(t + 1) * HEADS, :] = jnp.where(
                own_lanes, jnp.broadcast_to(q[t:t + 1, :], (HEADS, w)), 0.0)
            cn = cn + jnp.sum(jnp.where(lane_p == t, lfn, 0.0), axis=1, keepdims=True)
            cn_ref[t * HEADS:(t + 1) * HEADS, :] = cn
            cn_lanes = jnp.where(lane_p == t, cn, cn_lanes)
        cnl_ref[...] = cn_lanes
        m_ref[...] = jnp.full(m_ref.shape, NEG_INF, F32)
        l_ref[...] = jnp.zeros(l_ref.shape, F32)
        acc_ref[...] = jnp.zeros(acc_ref.shape, F32)
        carry_ref[...] = jnp.zeros(carry_ref.shape, F32)
        knp_ref[...] = jnp.zeros(knp_ref.shape, F32)
        vnp_ref[...] = jnp.zeros(vnp_ref.shape, F32)
        knp_ref[0:nt, :] = kn_ref[0]
        vnp_ref[0:nt, :] = vn_ref[0]

    qe = qe_ref[...].astype(BF16)
    cn = cn_ref[...]

    def online_update(z, pv):
        m_old = m_ref[...]
        m_new = jnp.maximum(m_old, jnp.max(z, axis=-1, keepdims=True) + cn)
        p = jnp.exp(z - (m_new - cn))
        alpha = jnp.exp(m_old - m_new)
        l_ref[...] = alpha * l_ref[...] + jnp.sum(p, axis=-1, keepdims=True)
        acc_ref[...] = alpha * acc_ref[...] + pv(p.astype(BF16))
        m_ref[...] = m_new

    lf = jnp.concatenate([r[...] for r in lf_refs], axis=1)
    lane_k = lax.broadcasted_iota(jnp.int32, lf.shape, 1)
    suf = lf
    d = 1
    while d < n_keys:
        suf = suf + jnp.where(lane_k < n_keys - d, pltpu.roll(suf, n_keys - d, axis=1), 0.0)
        d *= 2
    carry = carry_ref[:, 0:1]
    rc = (suf - lf) + carry
    carry_ref[...] = jnp.broadcast_to(carry + suf[:, 0:1], carry_ref.shape)
    kt_bf = jnp.concatenate([r[...].astype(BF16) for r in k_refs], axis=1)
    vt_bf = jnp.concatenate([r[...].astype(BF16) for r in v_refs], axis=1)
    z = _dot(qe, kt_bf) + jnp.concatenate([rc] * nt, axis=0)
    online_update(z, lambda p: _dot_nt(p, vt_bf))

    nxt_seq_off, nxt_j = divmod(j + n_slots, n_chunks)
    if nxt_seq_off == 0:
        for c in chunk_copies(seq, nxt_j):
            c.start()
    else:
        @pl.when(seq + nxt_seq_off < n_seq)
        def _():
            for c in chunk_copies(seq + nxt_seq_off, nxt_j):
                c.start()

    if j == n_chunks - 1:
        z_new = _dot_nt(qe, knp_ref[...].astype(BF16)) - jnp.concatenate([cnl_ref[...]] * nt, axis=0)
        r_id = lax.broadcasted_iota(jnp.int32, (rows, LANES), 0)
        c_id = lax.broadcasted_iota(jnp.int32, (rows, LANES), 1)
        z_new = jnp.where(c_id * HEADS <= r_id, z_new, NEG_INF)
        online_update(z_new, lambda p: _dot(p, vnp_ref[...].astype(BF16)))
        o = acc_ref[...] / l_ref[...]
        for t in range(nt):
            o_t = jnp.where(own_lanes, o[t * HEADS:(t + 1) * HEADS, :], 0.0)
            o_ref[0, t:t + 1, :] = jnp.sum(o_t, axis=0, keepdims=True)


def _attn_fused_kernel(pt_ref, q_ref, kt_ref, vt_ref, c_ref, ct_ref, qs_ref, kn_ref, vn_ref, lfn_ref,
                       ck_hbm, cv_hbm, clf_hbm, o_ref, os_ref, ck_ref, *decode_scratch,
                       tq, tk, n_pg, n_chunks, n_slots, nt):
    hp = pl.program_id(1)
    qi = pl.program_id(2)
    seq = (pl.program_id(0) * pl.num_programs(1) + hp) * pl.num_programs(2) + qi
    n_seq = pl.num_programs(0) * pl.num_programs(1) * pl.num_programs(2)
    for j in range(n_chunks):
        _decode_chunk(j, seq, n_seq, pt_ref, qs_ref, kn_ref, vn_ref, lfn_ref, ck_hbm, cv_hbm, clf_hbm,
                      os_ref, *decode_scratch, n_pg=n_pg, n_chunks=n_chunks, n_slots=n_slots, nt=nt)
    _attn_prompt_tile(q_ref, kt_ref, vt_ref, c_ref, ct_ref, o_ref, ck_ref, hp, qi, tq=tq, tk=tk)


def _attn_fused(q_bf, kt_bf, vt_bf, c_rows, c_t, page_table, q_s, k_new, v_new, lf_new_t,
                cache_k, cache_v, cache_lf_t, *, batch, seq, tq, tk, n_pg, n_slots):
    rows, w = q_bf.shape
    nq = seq // tq
    n_hp = w // LANES
    nb, nt, _ = q_s.shape
    n_chunks = page_table.shape[1] // n_pg
    page = cache_k.shape[2]
    assert page == LANES and cache_k.shape[1] == w
    assert nb == batch * n_hp * nq, "one sample sequence per prompt attention grid step"
    assert n_slots <= n_chunks
    srows = nt * HEADS

    def seq_spec(shape):
        return pl.BlockSpec((1,) + shape, lambda b, hp, qi, pt: ((b * n_hp + hp) * nq + qi, 0, 0))

    kv_spec = pl.BlockSpec((LANES, seq), lambda b, hp, qi, pt: (b * n_hp + hp, 0))
    q_spec = pl.BlockSpec((tq, LANES), lambda b, hp, qi, pt: (b * nq + qi, hp))
    in_specs = ([q_spec, kv_spec, kv_spec,
                 pl.BlockSpec((tq, LANES), lambda b, hp, qi, pt: (b * nq + qi, 0)),
                 pl.BlockSpec((HEADS, seq), lambda b, hp, qi, pt: (b, 0))]
                + [seq_spec((nt, w)) for _ in range(3)] + [seq_spec((HEADS, LANES))]
                + [pl.BlockSpec(memory_space=pl.ANY) for _ in range(3)])
    grid_spec = pltpu.PrefetchScalarGridSpec(
        num_scalar_prefetch=1, grid=(batch, n_hp, nq), in_specs=in_specs,
        out_specs=[q_spec, seq_spec((nt, w))],
        scratch_shapes=[pltpu.VMEM((1, seq), F32),
                        pltpu.VMEM((n_slots, n_pg, w, page), F32), pltpu.VMEM((n_slots, n_pg, w, page), F32),
                        pltpu.VMEM((n_slots, n_pg, HEADS, page), F32), pltpu.SemaphoreType.DMA((3, n_slots)),
                        pltpu.VMEM((srows, w), F32), pltpu.VMEM((srows, 1), F32),
                        pltpu.VMEM((HEADS, LANES), F32),
                        pltpu.VMEM((srows, 1), F32), pltpu.VMEM((srows, 1), F32),
                        pltpu.VMEM((srows, w), F32), pltpu.VMEM((HEADS, LANES), F32),
                        pltpu.VMEM((LANES, w), F32), pltpu.VMEM((LANES, w), F32)])
    return pl.pallas_call(
        functools.partial(_attn_fused_kernel, tq=tq, tk=tk, n_pg=n_pg, n_chunks=n_chunks,
                          n_slots=n_slots, nt=nt),
        grid_spec=grid_spec,
        out_shape=[jax.ShapeDtypeStruct((rows, w), F32), jax.ShapeDtypeStruct((nb, nt, w), F32)],
        compiler_params=pltpu.CompilerParams(
            dimension_semantics=("arbitrary", "arbitrary", "arbitrary"),
            vmem_limit_bytes=VMEM_LIMIT),
        name="attn_fused",
    )(page_table, q_bf, kt_bf, vt_bf, c_rows, c_t, q_s, k_new, v_new, lf_new_t,
      cache_k, cache_v, cache_lf_t)


def _ffn_math(x, ylru_bf, yfox, p, wr, ext_ref, fill_prev, keep_tail, *, pad, unit):
    (gfox_ref, wout_ref, gffn_ref, wup_ref, cw_ref, cb_ref, wdown_ref, wple_ref, wgate_ref,
     gfin_ref) = wr
    n = x.shape[0]
    wl = ylru_bf.shape[1]
    yf = _rms(yfox, gfox_ref[...]).astype(BF16)
    h = x + _dot(ylru_bf, wout_ref[0:wl, :]) + _dot(yf, wout_ref[wl:, :])
    n2 = _rms(h, gffn_ref[...]).astype(BF16)
    d_ff = wdown_ref.shape[0]
    cw = ext_ref.shape[1]
    acc = jnp.zeros(x.shape, F32)
    for c in range(d_ff // cw):
        cs = slice(c * cw, (c + 1) * cw)
        g = _dot(n2, wup_ref[:, cs])
        u = _dot(n2, wup_ref[:, d_ff + c * cw:d_ff + (c + 1) * cw])
        fill_prev(cs)
        ext_ref[pad:pad + n, :] = g
        gc = _causal_conv(ext_ref, cw_ref, cb_ref, cs, pad=pad, unit=unit, n=n)
        keep_tail(cs)
        acc = acc + _dot((_gelu(gc) * u).astype(BF16), wdown_ref[cs, :])
    h = h + acc
    e = _dot(p.astype(BF16), wple_ref[...]) * jax.nn.sigmoid(_dot(h.astype(BF16), wgate_ref[...]))
    return _rms(h + e, gfin_ref[...])


def _ffn_prompt_kernel(x_ref, ylru_ref, yfox_ref, p_ref, *rest, tm):
    wr = rest[:10]
    y_ref, tail_ref, ext_ref, carry_ref = rest[10:]
    t = pl.program_id(1)
    pad = SUBLANES
    taps = wr[4].shape[0]

    @pl.when(t == 0)
    def _():
        carry_ref[...] = jnp.zeros(carry_ref.shape, F32)

    def fill_prev(cs):
        ext_ref[0:pad, :] = carry_ref[:, cs]

    def keep_tail(cs):
        carry_ref[:, cs] = ext_ref[tm:tm + pad, :]
        tail_ref[0, :, cs] = ext_ref[pad + tm - (taps - 1):pad + tm, :]

    y_ref[...] = _ffn_math(x_ref[...], ylru_ref[...], yfox_ref[...], p_ref[...], wr, ext_ref,
                           fill_prev, keep_tail, pad=pad, unit=1)


def _ffn_prompt(x, ylru, yfox, p, weights, *, batch, seq, tm, conv_cols):
    rows, d = x.shape
    nt = seq // tm
    d_ff = weights[6].shape[0]
    taps = weights[4].shape[0]
    tile = lambda c: pl.BlockSpec((tm, c), lambda b, t: (b * nt + t, 0))
    return pl.pallas_call(
        functools.partial(_ffn_prompt_kernel, tm=tm),
        grid=(batch, nt),
        in_specs=[tile(d), tile(ylru.shape[1]), tile(yfox.shape[1]), tile(p.shape[1])]
        + [_const_spec(wt.shape) for wt in weights],
        out_specs=[tile(d), pl.BlockSpec((1, taps - 1, d_ff), lambda b, t: (b, 0, 0))],
        out_shape=[jax.ShapeDtypeStruct((rows, d), F32),
                   jax.ShapeDtypeStruct((batch, taps - 1, d_ff), F32)],
        scratch_shapes=[pltpu.VMEM((SUBLANES + tm, conv_cols), F32),
                        pltpu.VMEM((SUBLANES, d_ff), F32)],
        compiler_params=pltpu.CompilerParams(dimension_semantics=("arbitrary", "arbitrary"),
                                             vmem_limit_bytes=VMEM_LIMIT),
        name="ffn_prompt",
    )(x, ylru, yfox, p, *weights)


def _ffn_sample_kernel(x_ref, ylru_ref, yfox_ref, p_ref, st_ref, *rest, nt):
    wr = rest[:10]
    y_ref, tail_ref, ext_ref = rest[10:]
    nb = x_ref.shape[0]
    d = x_ref.shape[1] // nt
    d_ff = wr[6].shape[0]
    taps = wr[4].shape[0]
    pad = (taps - 1) * nb

    def fill_prev(cs):
        for j in range(taps - 1):
            ext_ref[j * nb:(j + 1) * nb, :] = st_ref[:, j * d_ff + cs.start:j * d_ff + cs.stop]

    def keep_tail(cs):
        for j in range(taps - 1):
            tail_ref[:, j * d_ff + cs.start:j * d_ff + cs.stop] = ext_ref[(nt + j) * nb:(nt + j + 1) * nb, :]

    y = _ffn_math(_time_major(x_ref, nt), _time_major(ylru_ref, nt), _time_major(yfox_ref, nt),
                  _time_major(p_ref, nt), wr, ext_ref, fill_prev, keep_tail, pad=pad, unit=nb)
    for t in range(nt):
        y_ref[:, t * d:(t + 1) * d] = y[t * nb:(t + 1) * nb, :]


def _ffn_sample(x, ylru, yfox, p, state, weights, *, nt, conv_cols):
    nb = x.shape[0]
    taps = weights[4].shape[0]
    args = [x, ylru, yfox, p, state] + list(weights)
    return pl.pallas_call(
        functools.partial(_ffn_sample_kernel, nt=nt),
        grid=(1,),
        in_specs=[_const_spec(a.shape) for a in args],
        out_specs=[_whole_spec(x.shape), _whole_spec(state.shape)],
        out_shape=[jax.ShapeDtypeStruct(x.shape, F32), jax.ShapeDtypeStruct(state.shape, F32)],
        scratch_shapes=[pltpu.VMEM(((taps - 1 + nt) * nb, conv_cols), F32)],
        compiler_params=pltpu.CompilerParams(dimension_semantics=("arbitrary",),
                                             vmem_limit_bytes=VMEM_LIMIT),
        name="ffn_sample",
    )(*args)


def _block_diag(w):
    nblk, k, j = w.shape
    eye = jnp.eye(nblk, dtype=w.dtype)
    return (w[:, :, None, :] * eye[:, None, :, None]).reshape(nblk * k, nblk * j)


def kernel(x_prompt, x_sample, p_prompt, p_sample, cache_k, cache_v, cache_logf, state_lru_h, state_lru_conv, state_ffn_conv, page_table, g_mix, w_in, lru_conv_w, lru_conv_b, lru_w_a, lru_b_a, lru_w_x, lru_b_x, lru_lambda, fox_b_f, g_lru_out, g_fox_out, w_out, g_ffn, w_up, ffn_conv_w, ffn_conv_b, w_down, w_ple, w_ple_gate, g_final):
    depth = w_in.shape[0]
    assert depth == 1, "single-layer step"
    bp, sp, d = x_prompt.shape
    bs, ts, _ = x_sample.shape
    lw = state_lru_h.shape[-1]
    fw = HEADS * HEAD_DIM
    d_ff = w_down.shape[1]
    n_phys, page = cache_k.shape[1], cache_k.shape[2]
    l = 0
    row = lambda a: a.reshape(1, -1)

    w_main = w_in[l][:, :2 * lw + 3 * fw].astype(BF16)
    w_f = jnp.pad(w_in[l][:, 2 * lw + 3 * fw:], ((0, 0), (0, LANES - HEADS))).astype(BF16)
    b_f = jnp.pad(fox_b_f[l], (0, LANES - HEADS)).reshape(1, LANES)
    lru_w = [lru_conv_w[l], row(lru_conv_b[l]), _block_diag(lru_w_a[l]).astype(BF16), row(lru_b_a[l]),
             _block_diag(lru_w_x[l]).astype(BF16), row(lru_b_x[l]), row(lru_lambda[l]), row(g_lru_out[l])]
    ffn_w = [row(g_fox_out[l]), w_out[l].astype(BF16), row(g_ffn[l]), w_up[l].astype(BF16),
             ffn_conv_w[l], row(ffn_conv_b[l]), w_down[l].astype(BF16), w_ple[l].astype(BF16),
             w_ple_gate[l].astype(BF16), row(g_final)]
    g_mix_r = row(g_mix[l])
    w_kvt = jnp.swapaxes(w_in[l][:, 2 * lw + fw:2 * lw + 3 * fw], 0, 1).astype(BF16)

    xp = x_prompt.reshape(bp * sp, d)
    xr, gr, q_bf, kt_p, vt_p, kt_bf, vt_bf, lf_pad = _inproj_prompt(
        xp, g_mix_r, w_main, w_kvt, w_f, b_f, batch=bp, seq=sp, tm=512)
    lft_p, c_rows, c_t = _cumsum(lf_pad, bp, sp)
    ylru_p, hlast_p = _lru_prompt(xr, gr, *lru_w, batch=bp, seq=sp, tt=512)
    to_bshd = lambda a: jnp.transpose(a.reshape(1, bp, HEADS, HEAD_DIM, sp), (0, 1, 4, 2, 3))
    k_p, v_p = to_bshd(kt_p), to_bshd(vt_p)
    logf_p = jnp.transpose(lft_p, (0, 2, 1))[None]

    xs = x_sample.reshape(bs, ts * d)
    xr_s, gr_s, q_s, k_s, v_s, lf_s = _inproj_sample(xs, g_mix_r, w_main, w_f, b_f, nt=ts)
    logf_s = lf_s.reshape(bs, ts, LANES)[:, :, :HEADS]
    ylru_s, hlast_s = _lru_sample(
        xr_s, gr_s, state_lru_h[l], state_lru_conv[l].reshape(bs, -1), *lru_w, nt=ts)

    pages_t = lambda c: jnp.transpose(c, (0, 2, 3, 1)).reshape(n_phys, fw, page)
    yfox_p, yfox_s = _attn_fused(
        q_bf, kt_bf, vt_bf, c_rows, c_t.reshape(bp * HEADS, sp),
        page_table, q_s.reshape(bs, ts, fw), k_s.reshape(bs, ts, fw), v_s.reshape(bs, ts, fw),
        jnp.pad(jnp.swapaxes(logf_s, 1, 2), ((0, 0), (0, 0), (0, LANES - ts))),
        pages_t(cache_k[l]), pages_t(cache_v[l]), jnp.swapaxes(cache_logf[l], 1, 2),
        batch=bp, seq=sp, tq=1024, tk=1024, n_pg=32, n_slots=2)
    y_p, ffn_tail_p = _ffn_prompt(xp, ylru_p, yfox_p, p_prompt[l].reshape(bp * sp, -1), ffn_w,
                                  batch=bp, seq=sp, tm=512, conv_cols=1536)
    y_s, ffn_tail_s = _ffn_sample(
        xs, ylru_s, yfox_s.reshape(bs, ts * fw), p_sample[l].reshape(bs, -1),
        state_ffn_conv[l].reshape(bs, -1), ffn_w, nt=ts, conv_cols=512)

    lru_taps = lru_conv_w.shape[1]
    return (y_p.reshape(bp, sp, d), y_s.reshape(bs, ts, d),
            k_p, v_p, logf_p, hlast_p.reshape(1, bp, lw),
            xr.reshape(bp, sp, lw)[None, :, sp - (lru_taps - 1):, :], ffn_tail_p[None],
            k_s.reshape(1, bs, ts, HEADS, HEAD_DIM), v_s.reshape(1, bs, ts, HEADS, HEAD_DIM),
            logf_s[None], hlast_s[None],
            xr_s.reshape(bs, ts, lw)[None, :, ts - (lru_taps - 1):, :],
            ffn_tail_s.reshape(1, bs, -1, d_ff))
```

```python
import functools
import math

import jax
import jax.numpy as jnp
from jax import lax
from jax.experimental import pallas as pl
from jax.experimental.pallas import tpu as pltpu

F32 = jnp.float32
BF16 = jnp.bfloat16

EPS = 1e-6
LRU_C = 8.0
HEAD_DIM = 64
HEADS = 8
LANES = 128
SUBLANES = 8
VMEM_LIMIT = 56 * 1024 * 1024
NEG_INF = float("-inf")
GELU_C = math.sqrt(2.0 / math.pi)
LOG2_E = math.log2(math.e)


def _rms(x, g):
    ms = jnp.mean(x * x, axis=-1, keepdims=True)
    return x * lax.rsqrt(ms + EPS) * g


def _gelu(x):
    return x * (0.5 * (1.0 + jnp.tanh(GELU_C * (x + 0.044715 * (x * x * x)))))


def _softplus(x):
    return jnp.maximum(x, 0.0) + jnp.log1p(jnp.exp(-jnp.abs(x)))


def _dot(a, b):
    return jnp.dot(a, b, preferred_element_type=F32)


def _dot_nt(a, b):
    return lax.dot_general(a, b, (((1,), (1,)), ((), ())), preferred_element_type=F32)


def _const_spec(shape):
    nd = len(shape)
    return pl.BlockSpec(shape, lambda *_: (0,) * nd, pipeline_mode=pl.Buffered(1))


def _whole_spec(shape):
    nd = len(shape)
    return pl.BlockSpec(shape, lambda *_: (0,) * nd)


def _time_major(ref, nt):
    c = ref.shape[1] // nt
    return jnp.concatenate([ref[:, t * c:(t + 1) * c] for t in range(nt)], axis=0)


def _inproj_sample_kernel(x_ref, g_ref, wm_ref, wf_ref, bf_ref,
                          xr_ref, gr_ref, q_ref, k_ref, v_ref, lf_ref, *, nt):
    d = x_ref.shape[1] // nt
    w = xr_ref.shape[1] // nt
    for t in range(nt):
        n = _rms(x_ref[:, t * d:(t + 1) * d], g_ref[...]).astype(BF16)
        ws = slice(t * w, (t + 1) * w)
        xr_ref[:, ws] = _dot(n, wm_ref[:, 0 * w:1 * w])
        gr_ref[:, ws] = _dot(n, wm_ref[:, 1 * w:2 * w])
        q_ref[:, ws] = _dot(n, wm_ref[:, 2 * w:3 * w]) * (HEAD_DIM ** -0.5)
        k_ref[:, ws] = _dot(n, wm_ref[:, 3 * w:4 * w])
        v_ref[:, ws] = _dot(n, wm_ref[:, 4 * w:5 * w])
        fl = _dot(n, wf_ref[...]) + bf_ref[...]
        lf_ref[:, t * LANES:(t + 1) * LANES] = -_softplus(-fl)


def _inproj_sample(x, g_mix, w_main, w_f, b_f, *, nt):
    rows = x.shape[0]
    w = w_main.shape[1] // 5
    args = [x, g_mix, w_main, w_f, b_f]
    shapes = [(rows, nt * w)] * 5 + [(rows, nt * LANES)]
    return pl.pallas_call(
        functools.partial(_inproj_sample_kernel, nt=nt),
        grid=(1,),
        in_specs=[_const_spec(a.shape) for a in args],
        out_specs=[_whole_spec(s) for s in shapes],
        out_shape=[jax.ShapeDtypeStruct(s, F32) for s in shapes],
        compiler_params=pltpu.CompilerParams(dimension_semantics=("arbitrary",),
                                             vmem_limit_bytes=VMEM_LIMIT),
        name="inproj_sample",
    )(*args)


def _inproj_prompt_kernel(x_ref, g_ref, wm_ref, wkvt_ref, wf_ref, bf_ref,
                          xr_ref, gr_ref, q_ref, kt_ref, vt_ref, ktb_ref, vtb_ref, lf_ref):
    n = _rms(x_ref[...], g_ref[...]).astype(BF16)
    w = xr_ref.shape[1]
    xr_ref[...] = _dot(n, wm_ref[:, 0 * w:1 * w])
    gr_ref[...] = _dot(n, wm_ref[:, 1 * w:2 * w])
    q_ref[...] = (_dot(n, wm_ref[:, 2 * w:3 * w]) * (HEAD_DIM ** -0.5 * LOG2_E)).astype(q_ref.dtype)
    kt = _dot_nt(wkvt_ref[0:w, :], n)
    vt = _dot_nt(wkvt_ref[w:2 * w, :], n)
    kt_ref[...] = kt
    vt_ref[...] = vt
    ktb_ref[...] = kt.astype(BF16)
    vtb_ref[...] = vt.astype(BF16)
    fl = _dot(n, wf_ref[...]) + bf_ref[...]
    lf_ref[...] = -_softplus(-fl)


def _inproj_prompt(x, g_mix, w_main, w_kvt, w_f, b_f, *, batch, seq, tm):
    rows, d = x.shape
    w = w_kvt.shape[0] // 2
    nt = seq // tm
    row_spec = lambda c: pl.BlockSpec((tm, c), lambda b, t: (b * nt + t, 0))
    col_spec = pl.BlockSpec((w, tm), lambda b, t: (b, t))
    consts = [g_mix, w_main, w_kvt, w_f, b_f]
    return pl.pallas_call(
        _inproj_prompt_kernel,
        grid=(batch, nt),
        in_specs=[row_spec(d)] + [_const_spec(c.shape) for c in consts],
        out_specs=[row_spec(w), row_spec(w), row_spec(w), col_spec, col_spec, col_spec, col_spec,
                   row_spec(LANES)],
        out_shape=[jax.ShapeDtypeStruct((rows, w), F32), jax.ShapeDtypeStruct((rows, w), F32),
                   jax.ShapeDtypeStruct((rows, w), BF16),
                   jax.ShapeDtypeStruct((batch * w, seq), F32), jax.ShapeDtypeStruct((batch * w, seq), F32),
                   jax.ShapeDtypeStruct((batch * w, seq), BF16), jax.ShapeDtypeStruct((batch * w, seq), BF16),
                   jax.ShapeDtypeStruct((rows, LANES), F32)],
        compiler_params=pltpu.CompilerParams(dimension_semantics=("arbitrary", "arbitrary"),
                                             vmem_limit_bytes=VMEM_LIMIT),
        name="inproj_prompt",
    )(x, *consts)


def _cumsum_kernel(lf_ref, lft_ref, c_ref, ct_ref):
    x = lf_ref[...]
    lft_ref[0] = x.T[:HEADS, :]
    n = x.shape[0]
    row = lax.broadcasted_iota(jnp.int32, x.shape, 0)
    d = 1
    while d < n:
        x = x + jnp.where(row >= d, pltpu.roll(x, d, axis=0), 0.0)
        d *= 2
    x = x * LOG2_E
    c_ref[...] = x
    ct_ref[0] = x.T[:HEADS, :]


def _cumsum(lf_pad, batch, seq):
    rows = lf_pad.shape[0]
    head_major = pl.BlockSpec((1, HEADS, seq), lambda b: (b, 0, 0))
    return pl.pallas_call(
        _cumsum_kernel,
        grid=(batch,),
        in_specs=[pl.BlockSpec((seq, LANES), lambda b: (b, 0))],
        out_specs=[head_major, pl.BlockSpec((seq, LANES), lambda b: (b, 0)), head_major],
        out_shape=[jax.ShapeDtypeStruct((batch, HEADS, seq), F32),
                   jax.ShapeDtypeStruct((rows, LANES), F32),
                   jax.ShapeDtypeStruct((batch, HEADS, seq), F32)],
        compiler_params=pltpu.CompilerParams(dimension_semantics=("arbitrary",),
                                             vmem_limit_bytes=VMEM_LIMIT),
        name="logf_cumsum",
    )(lf_pad)


def _causal_conv(ext_ref, cw_ref, cb_ref, cols, *, pad, unit, n):
    taps = cw_ref.shape[0]
    out = cb_ref[:, cols]
    for j in range(taps):
        out = out + cw_ref[taps - 1 - j:taps - j, cols] * ext_ref[pl.ds(pad - j * unit, n), :]
    return out


def _lru_gates(xc, wa_ref, ba_ref, wx_ref, bx_ref, lam_ref):
    xcb = xc.astype(BF16)
    r = jax.nn.sigmoid(_dot(xcb, wa_ref[...]) + ba_ref[...])
    i = jax.nn.sigmoid(_dot(xcb, wx_ref[...]) + bx_ref[...])
    log_a = (-LRU_C) * r * _softplus(-lam_ref[...])
    a = jnp.exp(log_a)
    mult = jnp.sqrt(-jnp.tanh(log_a) * (a * a + 1.0))
    return a, mult, i * xc


def _lru_prompt_kernel(xr_ref, gr_ref, cw_ref, cb_ref, wa_ref, ba_ref, wx_ref, bx_ref, lam_ref,
                       g_ref, y_ref, hlast_ref, ext_ref, a_ref, u_ref, hs_ref, h_ref, *, tt):
    t = pl.program_id(1)
    pad = SUBLANES

    @pl.when(t == 0)
    def _():
        ext_ref[0:pad, :] = jnp.zeros((pad, ext_ref.shape[1]), F32)
        h_ref[...] = jnp.zeros(h_ref.shape, F32)

    x = xr_ref[...]
    ext_ref[pad:pad + tt, :] = x
    xc = _causal_conv(ext_ref, cw_ref, cb_ref, slice(None), pad=pad, unit=1, n=tt)
    ext_ref[0:pad, :] = x[tt - pad:tt, :]

    a, mult, ix = _lru_gates(xc, wa_ref, ba_ref, wx_ref, bx_ref, lam_ref)
    row = lax.broadcasted_iota(jnp.int32, (tt, 1), 0)
    mult = jnp.where((row == 0) & (t == 0), 1.0, mult)
    a_ref[...] = a
    u_ref[...] = mult * ix

    row8 = lax.broadcasted_iota(jnp.int32, (SUBLANES, a_ref.shape[1]), 0)

    def group(gi, h):
        r0 = pl.multiple_of(gi * SUBLANES, SUBLANES)
        ag = a_ref[pl.ds(r0, SUBLANES), :]
        ug = u_ref[pl.ds(r0, SUBLANES), :]
        for d in (1, 2, 4):
            a_sh = jnp.where(row8 >= d, pltpu.roll(ag, d, axis=0), 1.0)
            u_sh = jnp.where(row8 >= d, pltpu.roll(ug, d, axis=0), 0.0)
            ug = ag * u_sh + ug
            ag = ag * a_sh
        hs = ag * h + ug
        hs_ref[pl.ds(r0, SUBLANES), :] = hs
        return jnp.broadcast_to(hs[SUBLANES - 1:SUBLANES, :], hs.shape)

    h = lax.fori_loop(0, tt // SUBLANES, group, h_ref[...], unroll=2)
    h_ref[...] = h
    hlast_ref[0] = h[0:1, :]
    y = hs_ref[...] * _gelu(gr_ref[...])
    y_ref[...] = _rms(y, g_ref[...]).astype(y_ref.dtype)


def _lru_prompt(xr, gr, cw, cb, wa, ba, wx, bx, lam, g_out, *, batch, seq, tt):
    rows, w = xr.shape
    nt = seq // tt
    tile = pl.BlockSpec((tt, w), lambda b, t: (b * nt + t, 0))
    consts = [cw, cb, wa, ba, wx, bx, lam, g_out]
    return pl.pallas_call(
        functools.partial(_lru_prompt_kernel, tt=tt),
        grid=(batch, nt),
        in_specs=[tile, tile] + [_const_spec(c.shape) for c in consts],
        out_specs=[tile, pl.BlockSpec((1, 1, w), lambda b, t: (b, 0, 0))],
        out_shape=[jax.ShapeDtypeStruct((rows, w), BF16), jax.ShapeDtypeStruct((batch, 1, w), F32)],
        scratch_shapes=[pltpu.VMEM((SUBLANES + tt, w), F32), pltpu.VMEM((tt, w), F32),
                        pltpu.VMEM((tt, w), F32), pltpu.VMEM((tt, w), F32),
                        pltpu.VMEM((SUBLANES, w), F32)],
        compiler_params=pltpu.CompilerParams(dimension_semantics=("arbitrary", "arbitrary"),
                                             vmem_limit_bytes=VMEM_LIMIT),
        name="lru_prompt",
    )(xr, gr, *consts)


def _lru_sample_kernel(xr_ref, gr_ref, h0_ref, cbuf_ref, cw_ref, cb_ref, wa_ref, ba_ref, wx_ref,
                       bx_ref, lam_ref, g_ref, y_ref, hlast_ref, ext_ref, *, nt):
    nb, w = h0_ref.shape
    taps = cw_ref.shape[0]
    pad = (taps - 1) * nb
    n = nb * nt
    ext_ref[0:pad, :] = _time_major(cbuf_ref, taps - 1)
    ext_ref[pad:pad + n, :] = _time_major(xr_ref, nt)
    xc = _causal_conv(ext_ref, cw_ref, cb_ref, slice(None), pad=pad, unit=nb, n=n)
    a, mult, ix = _lru_gates(xc, wa_ref, ba_ref, wx_ref, bx_ref, lam_ref)
    u = mult * ix
    h = h0_ref[...]
    for t in range(nt):
        h = a[t * nb:(t + 1) * nb, :] * h + u[t * nb:(t + 1) * nb, :]
        y = h * _gelu(gr_ref[:, t * w:(t + 1) * w])
        y_ref[:, t * w:(t + 1) * w] = _rms(y, g_ref[...]).astype(y_ref.dtype)
    hlast_ref[...] = h


def _lru_sample(xr, gr, h0, cbuf, cw, cb, wa, ba, wx, bx, lam, g_out, *, nt):
    nb, w = h0.shape
    args = [xr, gr, h0, cbuf, cw, cb, wa, ba, wx, bx, lam, g_out]
    taps = cw.shape[0]
    return pl.pallas_call(
        functools.partial(_lru_sample_kernel, nt=nt),
        grid=(1,),
        in_specs=[_const_spec(a.shape) for a in args],
        out_specs=[_whole_spec((nb, nt * w)), _whole_spec((nb, w))],
        out_shape=[jax.ShapeDtypeStruct((nb, nt * w), BF16), jax.ShapeDtypeStruct((nb, w), F32)],
        scratch_shapes=[pltpu.VMEM(((taps - 1 + nt) * nb, w), F32)],
        compiler_params=pltpu.CompilerParams(dimension_semantics=("arbitrary",),
                                             vmem_limit_bytes=VMEM_LIMIT),
        name="lru_sample",
    )(*args)


def _attn_prompt_tile(q_ref, kt_ref, vt_ref, c_ref, ct_ref, o_ref, ck_ref, hp, qi, *, tq, tk):
    heads_per_blk = LANES // HEAD_DIM
    q = q_ref[...]
    c_blk = c_ref[...]
    lane = lax.broadcasted_iota(jnp.int32, (tq, LANES), 1)
    n_full = (qi * tq) // tk
    n_diag = pl.cdiv(tq, tk)
    outs = []
    for hh in range(heads_per_blk):
        h = hp * heads_per_blk + hh
        in_head = (lane >= hh * HEAD_DIM) & (lane < (hh + 1) * HEAD_DIM)
        qm = jnp.where(in_head, q, jnp.zeros_like(q))
        cq = jnp.sum(jnp.where(lane == h, c_blk, 0.0), axis=-1, keepdims=True)
        ck_ref[...] = ct_ref[pl.ds(h, 1), :]

        def tile(j, carry, masked):
            m, l, acc = carry
            ks = pl.multiple_of(j * tk, tk)
            z = _dot(qm, kt_ref[:, pl.ds(ks, tk)]) - ck_ref[:, pl.ds(ks, tk)]
            if masked:
                row_i = lax.broadcasted_iota(jnp.int32, (tq, tk), 0) + qi * tq
                col_i = lax.broadcasted_iota(jnp.int32, (tq, tk), 1) + j * tk
                z = jnp.where(col_i <= row_i, z, NEG_INF)
            m_new = jnp.maximum(m, jnp.max(z, axis=-1, keepdims=True) + cq)
            p = jnp.exp2(z - (m_new - cq))
            alpha = jnp.exp2(m - m_new)
            l = alpha * l + jnp.sum(p, axis=-1, keepdims=True)
            acc = alpha * acc + _dot_nt(p.astype(BF16), vt_ref[:, pl.ds(ks, tk)])
            return m_new, l, acc

        carry = (jnp.full((tq, 1), NEG_INF, F32), jnp.zeros((tq, 1), F32), jnp.zeros((tq, LANES), F32))
        carry = lax.fori_loop(0, n_full, functools.partial(tile, masked=False), carry)
        for dj in range(n_diag):
            carry = tile(n_full + dj, carry, True)
        m, l, acc = carry
        outs.append((in_head, acc / l))
    o_ref[...] = jnp.where(outs[0][0], outs[0][1], outs[1][1])


def _decode_chunk(j, seq, n_seq, pt_ref, q_ref, kn_ref, vn_ref, lfn_ref, ck_hbm, cv_hbm, clf_hbm, o_ref,
                  kbuf, vbuf, lfbuf, sem, qe_ref, cn_ref, cnl_ref, m_ref, l_ref, acc_ref,
                  carry_ref, knp_ref, vnp_ref, *, n_pg, n_chunks, n_slots, nt):
    def chunk_copies(s, jj):
        buf_slot = (s * n_chunks + jj) % n_slots
        first_page = (n_chunks - 1 - jj) * n_pg
        copies = []
        for i in range(n_pg):
            pg = pt_ref[s, first_page + i]
            copies.append(pltpu.make_async_copy(ck_hbm.at[pg], kbuf.at[buf_slot, i], sem.at[0, buf_slot]))
            copies.append(pltpu.make_async_copy(cv_hbm.at[pg], vbuf.at[buf_slot, i], sem.at[1, buf_slot]))
            copies.append(pltpu.make_async_copy(clf_hbm.at[pg], lfbuf.at[buf_slot, i], sem.at[2, buf_slot]))
        return copies

    def start_all(copies):
        for k, c in enumerate(copies):
            c.start(priority=(k // 3) % 2)

    if j == 0:
        @pl.when(seq == 0)
        def _():
            for jj in range(n_slots):
                start_all(chunk_copies(0, jj))

    for c in chunk_copies(seq, j):
        c.wait()
    slot = (seq * n_chunks + j) % n_slots
    k_refs = [kbuf.at[slot, i] for i in range(n_pg)]
    v_refs = [vbuf.at[slot, i] for i in range(n_pg)]
    lf_refs = [lfbuf.at[slot, i] for i in range(n_pg)]
    w = q_ref.shape[2]
    n_keys = n_pg * LANES
    rows = nt * HEADS
    lane_p = lax.broadcasted_iota(jnp.int32, (HEADS, LANES), 1)
    sub = lax.broadcasted_iota(jnp.int32, (HEADS, w), 0)
    lane_w = lax.broadcasted_iota(jnp.int32, (HEADS, w), 1)
    own_lanes = (lane_w >= sub * HEAD_DIM) & (lane_w < (sub + 1) * HEAD_DIM)

    if j == 0:
        q = q_ref[0]
        lfn = lfn_ref[0]
        cn = jnp.zeros((HEADS, 1), F32)
        cn_lanes = jnp.zeros((HEADS, LANES), F32)
        for t in range(nt):
            qe_ref[t * HEADS:(t + 1) * HEADS, :] = jnp.where(
                own_lanes, jnp.broadcast_to(q[t:t + 1, :], (HEADS, w)), 0.0)
            cn = cn + jnp.sum(jnp.where(lane_p == t, lfn, 0.0), axis=1, keepdims=True)
            cn_ref[t * HEADS:(t + 1) * HEADS, :] = cn
            cn_lanes = jnp.where(lane_p == t, cn, cn_lanes)
        cnl_ref[...] = cn_lanes
        m_ref[...] = jnp.full(m_ref.shape, NEG_INF, F32)
        l_ref[...] = jnp.zeros(l_ref.shape, F32)
        acc_ref[...] = jnp.zeros(acc_ref.shape, F32)
        carry_ref[...] = jnp.zeros(carry_ref.shape, F32)
        knp_ref[...] = jnp.zeros(knp_ref.shape, F32)
        vnp_ref[...] = jnp.zeros(vnp_ref.shape, F32)
        knp_ref[0:nt, :] = kn_ref[0]
        vnp_ref[0:nt, :] = vn_ref[0]

    qe = qe_ref[...].astype(BF16)
    cn = cn_ref[...]

    def online_update(z, pv):
        m_old = m_ref[...]
        m_new = jnp.maximum(m_old, jnp.max(z, axis=-1, keepdims=True) + cn)
        p = jnp.exp(z - (m_new - cn))
        alpha = jnp.exp(m_old - m_new)
        l_ref[...] = alpha * l_ref[...] + jnp.sum(p, axis=-1, keepdims=True)
        acc_ref[...] = alpha * acc_ref[...] + pv(p.astype(BF16))
        m_ref[...] = m_new

    lf = jnp.concatenate([r[...] for r in lf_refs], axis=1)
    lane_k = lax.broadcasted_iota(jnp.int32, lf.shape, 1)
    suf = lf
    d = 1
    while d < n_keys:
        suf = suf + jnp.where(lane_k < n_keys - d, pltpu.roll(suf, n_keys - d, axis=1), 0.0)
        d *= 2
    carry = carry_ref[:, 0:1]
    rc = (suf - lf) + carry
    carry_ref[...] = jnp.broadcast_to(carry + suf[:, 0:1], carry_ref.shape)
    kt_bf = jnp.concatenate([r[...].astype(BF16) for r in k_refs], axis=1)
    vt_bf = jnp.concatenate([r[...].astype(BF16) for r in v_refs], axis=1)
    z = _dot(qe, kt_bf) + jnp.concatenate([rc] * nt, axis=0)
    online_update(z, lambda p: _dot_nt(p, vt_bf))

    nxt_seq_off, nxt_j = divmod(j + n_slots, n_chunks)
    if nxt_seq_off == 0:
        start_all(chunk_copies(seq, nxt_j))
    else:
        @pl.when(seq + nxt_seq_off < n_seq)
        def _():
            start_all(chunk_copies(seq + nxt_seq_off, nxt_j))

    if j == n_chunks - 1:
        z_new = _dot_nt(qe, knp_ref[...].astype(BF16)) - jnp.concatenate([cnl_ref[...]] * nt, axis=0)
        r_id = lax.broadcasted_iota(jnp.int32, (rows, LANES), 0)
        c_id = lax.broadcasted_iota(jnp.int32, (rows, LANES), 1)
        z_new = jnp.where(c_id * HEADS <= r_id, z_new, NEG_INF)
        online_update(z_new, lambda p: _dot(p, vnp_ref[...].astype(BF16)))
        o = acc_ref[...] / l_ref[...]
        for t in range(nt):
            o_t = jnp.where(own_lanes, o[t * HEADS:(t + 1) * HEADS, :], 0.0)
            o_ref[0, t:t + 1, :] = jnp.sum(o_t, axis=0, keepdims=True)


def _attn_fused_kernel(pt_ref, q_ref, kt_ref, vt_ref, c_ref, ct_ref, qs_ref, kn_ref, vn_ref, lfn_ref,
                       ck_hbm, cv_hbm, clf_hbm, o_ref, os_ref, ck_ref, *decode_scratch,
                       tq, tk, n_pg, n_chunks, n_slots, nt):
    hp = pl.program_id(1)
    qi = pl.program_id(2)
    seq = (pl.program_id(0) * pl.num_programs(1) + hp) * pl.num_programs(2) + qi
    n_seq = pl.num_programs(0) * pl.num_programs(1) * pl.num_programs(2)
    for j in range(n_chunks):
        _decode_chunk(j, seq, n_seq, pt_ref, qs_ref, kn_ref, vn_ref, lfn_ref, ck_hbm, cv_hbm, clf_hbm,
                      os_ref, *decode_scratch, n_pg=n_pg, n_chunks=n_chunks, n_slots=n_slots, nt=nt)
    _attn_prompt_tile(q_ref, kt_ref, vt_ref, c_ref, ct_ref, o_ref, ck_ref, hp, qi, tq=tq, tk=tk)


def _attn_fused(q_bf, kt_bf, vt_bf, c_rows, c_t, page_table, q_s, k_new, v_new, lf_new_t,
                cache_k, cache_v, cache_lf_t, *, batch, seq, tq, tk, n_pg, n_slots):
    rows, w = q_bf.shape
    nq = seq // tq
    n_hp = w // LANES
    nb, nt, _ = q_s.shape
    n_chunks = page_table.shape[1] // n_pg
    page = cache_k.shape[2]
    assert page == LANES and cache_k.shape[1] == w
    assert nb == batch * n_hp * nq, "one sample sequence per prompt attention grid step"
    assert n_slots <= n_chunks
    srows = nt * HEADS

    def seq_spec(shape):
        return pl.BlockSpec((1,) + shape, lambda b, hp, qi, pt: ((b * n_hp + hp) * nq + qi, 0, 0))

    kv_spec = pl.BlockSpec((LANES, seq), lambda b, hp, qi, pt: (b * n_hp + hp, 0))
    q_spec = pl.BlockSpec((tq, LANES), lambda b, hp, qi, pt: (b * nq + qi, hp))
    in_specs = ([q_spec, kv_spec, kv_spec,
                 pl.BlockSpec((tq, LANES), lambda b, hp, qi, pt: (b * nq + qi, 0)),
                 pl.BlockSpec((HEADS, seq), lambda b, hp, qi, pt: (b, 0))]
                + [seq_spec((nt, w)) for _ in range(3)] + [seq_spec((HEADS, LANES))]
                + [pl.BlockSpec(memory_space=pl.ANY) for _ in range(3)])
    grid_spec = pltpu.PrefetchScalarGridSpec(
        num_scalar_prefetch=1, grid=(batch, n_hp, nq), in_specs=in_specs,
        out_specs=[q_spec, seq_spec((nt, w))],
        scratch_shapes=[pltpu.VMEM((1, seq), F32),
                        pltpu.VMEM((n_slots, n_pg, w, page), F32), pltpu.VMEM((n_slots, n_pg, w, page), F32),
                        pltpu.VMEM((n_slots, n_pg, HEADS, page), F32), pltpu.SemaphoreType.DMA((3, n_slots)),
                        pltpu.VMEM((srows, w), F32), pltpu.VMEM((srows, 1), F32),
                        pltpu.VMEM((HEADS, LANES), F32),
                        pltpu.VMEM((srows, 1), F32), pltpu.VMEM((srows, 1), F32),
                        pltpu.VMEM((srows, w), F32), pltpu.VMEM((HEADS, LANES), F32),
                        pltpu.VMEM((LANES, w), F32), pltpu.VMEM((LANES, w), F32)])
    return pl.pallas_call(
        functools.partial(_attn_fused_kernel, tq=tq, tk=tk, n_pg=n_pg, n_chunks=n_chunks,
                          n_slots=n_slots, nt=nt),
        grid_spec=grid_spec,
        out_shape=[jax.ShapeDtypeStruct((rows, w), F32), jax.ShapeDtypeStruct((nb, nt, w), F32)],
        compiler_params=pltpu.CompilerParams(
            dimension_semantics=("arbitrary", "arbitrary", "arbitrary"),
            vmem_limit_bytes=VMEM_LIMIT),
        name="attn_fused",
    )(page_table, q_bf, kt_bf, vt_bf, c_rows, c_t, q_s, k_new, v_new, lf_new_t,
      cache_k, cache_v, cache_lf_t)


def _ffn_math(x, ylru_bf, yfox, p, wr, ext_ref, fill_prev, keep_tail, *, pad, unit):
    (gfox_ref, wout_ref, gffn_ref, wup_ref, cw_ref, cb_ref, wdown_ref, wple_ref, wgate_ref,
     gfin_ref) = wr
    n = x.shape[0]
    wl = ylru_bf.shape[1]
    yf = _rms(yfox, gfox_ref[...]).astype(BF16)
    h = x + _dot(ylru_bf, wout_ref[0:wl, :]) + _dot(yf, wout_ref[wl:, :])
    n2 = _rms(h, gffn_ref[...]).astype(BF16)
    d_ff = wdown_ref.shape[0]
    cw = ext_ref.shape[1]
    acc = jnp.zeros(x.shape, F32)
    for c in range(d_ff // cw):
        cs = slice(c * cw, (c + 1) * cw)
        g = _dot(n2, wup_ref[:, cs])
        u = _dot(n2, wup_ref[:, d_ff + c * cw:d_ff + (c + 1) * cw])
        fill_prev(cs)
        ext_ref[pad:pad + n, :] = g
        gc = _causal_conv(ext_ref, cw_ref, cb_ref, cs, pad=pad, unit=unit, n=n)
        keep_tail(cs)
        acc = acc + _dot((_gelu(gc) * u).astype(BF16), wdown_ref[cs, :])
    h = h + acc
    e = _dot(p.astype(BF16), wple_ref[...]) * jax.nn.sigmoid(_dot(h.astype(BF16), wgate_ref[...]))
    return _rms(h + e, gfin_ref[...])


def _ffn_prompt_kernel(x_ref, ylru_ref, yfox_ref, p_ref, *rest, tm):
    wr = rest[:10]
    y_ref, tail_ref, ext_ref, carry_ref = rest[10:]
    t = pl.program_id(1)
    pad = SUBLANES
    taps = wr[4].shape[0]

    @pl.when(t == 0)
    def _():
        carry_ref[...] = jnp.zeros(carry_ref.shape, F32)

    def fill_prev(cs):
        ext_ref[0:pad, :] = carry_ref[:, cs]

    def keep_tail(cs):
        carry_ref[:, cs] = ext_ref[tm:tm + pad, :]
        tail_ref[0, :, cs] = ext_ref[pad + tm - (taps - 1):pad + tm, :]

    y_ref[...] = _ffn_math(x_ref[...], ylru_ref[...], yfox_ref[...], p_ref[...], wr, ext_ref,
                           fill_prev, keep_tail, pad=pad, unit=1)


def _ffn_prompt(x, ylru, yfox, p, weights, *, batch, seq, tm, conv_cols):
    rows, d = x.shape
    nt = seq // tm
    d_ff = weights[6].shape[0]
    taps = weights[4].shape[0]
    tile = lambda c: pl.BlockSpec((tm, c), lambda b, t: (b * nt + t, 0))
    return pl.pallas_call(
        functools.partial(_ffn_prompt_kernel, tm=tm),
        grid=(batch, nt),
        in_specs=[tile(d), tile(ylru.shape[1]), tile(yfox.shape[1]), tile(p.shape[1])]
        + [_const_spec(wt.shape) for wt in weights],
        out_specs=[tile(d), pl.BlockSpec((1, taps - 1, d_ff), lambda b, t: (b, 0, 0))],
        out_shape=[jax.ShapeDtypeStruct((rows, d), F32),
                   jax.ShapeDtypeStruct((batch, taps - 1, d_ff), F32)],
        scratch_shapes=[pltpu.VMEM((SUBLANES + tm, conv_cols), F32),
                        pltpu.VMEM((SUBLANES, d_ff), F32)],
        compiler_params=pltpu.CompilerParams(dimension_semantics=("arbitrary", "arbitrary"),
                                             vmem_limit_bytes=VMEM_LIMIT),
        name="ffn_prompt",
    )(x, ylru, yfox, p, *weights)


def _ffn_sample_kernel(x_ref, ylru_ref, yfox_ref, p_ref, st_ref, *rest, nt):
    wr = rest[:10]
    y_ref, tail_ref, ext_ref = rest[10:]
    nb = x_ref.shape[0]
    d = x_ref.shape[1] // nt
    d_ff = wr[6].shape[0]
    taps = wr[4].shape[0]
    pad = (taps - 1) * nb

    def fill_prev(cs):
        for j in range(taps - 1):
            ext_ref[j * nb:(j + 1) * nb, :] = st_ref[:, j * d_ff + cs.start:j * d_ff + cs.stop]

    def keep_tail(cs):
        for j in range(taps - 1):
            tail_ref[:, j * d_ff + cs.start:j * d_ff + cs.stop] = ext_ref[(nt + j) * nb:(nt + j + 1) * nb, :]

    y = _ffn_math(_time_major(x_ref, nt), _time_major(ylru_ref, nt), _time_major(yfox_ref, nt),
                  _time_major(p_ref, nt), wr, ext_ref, fill_prev, keep_tail, pad=pad, unit=nb)
    for t in range(nt):
        y_ref[:, t * d:(t + 1) * d] = y[t * nb:(t + 1) * nb, :]


def _ffn_sample(x, ylru, yfox, p, state, weights, *, nt, conv_cols):
    nb = x.shape[0]
    taps = weights[4].shape[0]
    args = [x, ylru, yfox, p, state] + list(weights)
    return pl.pallas_call(
        functools.partial(_ffn_sample_kernel, nt=nt),
        grid=(1,),
        in_specs=[_const_spec(a.shape) for a in args],
        out_specs=[_whole_spec(x.shape), _whole_spec(state.shape)],
        out_shape=[jax.ShapeDtypeStruct(x.shape, F32), jax.ShapeDtypeStruct(state.shape, F32)],
        scratch_shapes=[pltpu.VMEM(((taps - 1 + nt) * nb, conv_cols), F32)],
        compiler_params=pltpu.CompilerParams(dimension_semantics=("arbitrary",),
                                             vmem_limit_bytes=VMEM_LIMIT),
        name="ffn_sample",
    )(*args)


def _block_diag(w):
    nblk, k, j = w.shape
    eye = jnp.eye(nblk, dtype=w.dtype)
    return (w[:, :, None, :] * eye[:, None, :, None]).reshape(nblk * k, nblk * j)


def kernel(x_prompt, x_sample, p_prompt, p_sample, cache_k, cache_v, cache_logf, state_lru_h, state_lru_conv, state_ffn_conv, page_table, g_mix, w_in, lru_conv_w, lru_conv_b, lru_w_a, lru_b_a, lru_w_x, lru_b_x, lru_lambda, fox_b_f, g_lru_out, g_fox_out, w_out, g_ffn, w_up, ffn_conv_w, ffn_conv_b, w_down, w_ple, w_ple_gate, g_final):
    depth = w_in.shape[0]
    assert depth == 1, "single-layer step"
    bp, sp, d = x_prompt.shape
    bs, ts, _ = x_sample.shape
    lw = state_lru_h.shape[-1]
    fw = HEADS * HEAD_DIM
    d_ff = w_down.shape[1]
    n_phys, page = cache_k.shape[1], cache_k.shape[2]
    l = 0
    row = lambda a: a.reshape(1, -1)

    w_main = w_in[l][:, :2 * lw + 3 * fw].astype(BF16)
    w_f = jnp.pad(w_in[l][:, 2 * lw + 3 * fw:], ((0, 0), (0, LANES - HEADS))).astype(BF16)
    b_f = jnp.pad(fox_b_f[l], (0, LANES - HEADS)).reshape(1, LANES)
    lru_w = [lru_conv_w[l], row(lru_conv_b[l]), _block_diag(lru_w_a[l]).astype(BF16), row(lru_b_a[l]),
             _block_diag(lru_w_x[l]).astype(BF16), row(lru_b_x[l]), row(lru_lambda[l]), row(g_lru_out[l])]
    ffn_w = [row(g_fox_out[l]), w_out[l].astype(BF16), row(g_ffn[l]), w_up[l].astype(BF16),
             ffn_conv_w[l], row(ffn_conv_b[l]), w_down[l].astype(BF16), w_ple[l].astype(BF16),
             w_ple_gate[l].astype(BF16), row(g_final)]
    g_mix_r = row(g_mix[l])
    w_kvt = jnp.swapaxes(w_in[l][:, 2 * lw + fw:2 * lw + 3 * fw], 0, 1).astype(BF16)

    xp = x_prompt.reshape(bp * sp, d)
    xr, gr, q_bf, kt_p, vt_p, kt_bf, vt_bf, lf_pad = _inproj_prompt(
        xp, g_mix_r, w_main, w_kvt, w_f, b_f, batch=bp, seq=sp, tm=512)
    lft_p, c_rows, c_t = _cumsum(lf_pad, bp, sp)
    ylru_p, hlast_p = _lru_prompt(xr, gr, *lru_w, batch=bp, seq=sp, tt=512)
    to_bshd = lambda a: jnp.transpose(a.reshape(1, bp, HEADS, HEAD_DIM, sp), (0, 1, 4, 2, 3))
    k_p, v_p = to_bshd(kt_p), to_bshd(vt_p)
    logf_p = jnp.transpose(lft_p, (0, 2, 1))[None]

    xs = x_sample.reshape(bs, ts * d)
    xr_s, gr_s, q_s, k_s, v_s, lf_s = _inproj_sample(xs, g_mix_r, w_main, w_f, b_f, nt=ts)
    logf_s = lf_s.reshape(bs, ts, LANES)[:, :, :HEADS]
    ylru_s, hlast_s = _lru_sample(
        xr_s, gr_s, state_lru_h[l], state_lru_conv[l].reshape(bs, -1), *lru_w, nt=ts)

    pages_t = lambda c: jnp.transpose(c, (0, 2, 3, 1)).reshape(n_phys, fw, page)
    yfox_p, yfox_s = _attn_fused(
        q_bf, kt_bf, vt_bf, c_rows, c_t.reshape(bp * HEADS, sp),
        page_table, q_s.reshape(bs, ts, fw), k_s.reshape(bs, ts, fw), v_s.reshape(bs, ts, fw),
        jnp.pad(jnp.swapaxes(logf_s, 1, 2), ((0, 0), (0, 0), (0, LANES - ts))),
        pages_t(cache_k[l]), pages_t(cache_v[l]), jnp.swapaxes(cache_logf[l], 1, 2),
        batch=bp, seq=sp, tq=1024, tk=1024, n_pg=32, n_slots=2)
    y_p, ffn_tail_p = _ffn_prompt(xp, ylru_p, yfox_p, p_prompt[l].reshape(bp * sp, -1), ffn_w,
                                  batch=bp, seq=sp, tm=512, conv_cols=1536)
    y_s, ffn_tail_s = _ffn_sample(
        xs, ylru_s, yfox_s.reshape(bs, ts * fw), p_sample[l].reshape(bs, -1),
        state_ffn_conv[l].reshape(bs, -1), ffn_w, nt=ts, conv_cols=512)

    lru_taps = lru_conv_w.shape[1]
    return (y_p.reshape(bp, sp, d), y_s.reshape(bs, ts, d),
            k_p, v_p, logf_p, hlast_p.reshape(1, bp, lw),
            xr.reshape(bp, sp, lw)[None, :, sp - (lru_taps - 1):, :], ffn_tail_p[None],
            k_s.reshape(1, bs, ts, HEADS, HEAD_DIM), v_s.reshape(1, bs, ts, HEADS, HEAD_DIM),
            logf_s[None], hlast_s[None],
            xr_s.reshape(bs, ts, lw)[None, :, ts - (lru_taps - 1):, :],
            ffn_tail_s.reshape(1, bs, -1, d_ff))
```
